```python
import jax
import jax.numpy as jnp
from jax import lax
import numpy as np

D_MODEL = 1024
BATCH = 4
SEQ = 8192
DEPTH = 4

MEM_LEN = 256
EPS = 1e-6
F32 = jnp.float32

GLA_HEADS = 6
GLA_DK = 32
GLA_DV = 64
GLA_RANK = 16
GLA_TAU = 16.0
GLA_CHUNK = 64

RET_HEADS = 6
RET_DK = 32
RET_DV = 64
RET_CHUNK = 128
ROPE_BASE = 10000.0

ML_HEADS = 4
ML_DK = 64
ML_DV = 64
ML_CHUNK = 64
ML_CONV = 4
ML_FBIAS_LO = 3.0
ML_FBIAS_HI = 6.0

D_MIX = GLA_HEADS * GLA_DV + RET_HEADS * RET_DV + ML_HEADS * ML_DV
IN_SIZES = (
    GLA_HEADS * GLA_DK, GLA_HEADS * GLA_DK, GLA_HEADS * GLA_DV, GLA_RANK, GLA_HEADS * GLA_DV,
    RET_HEADS * RET_DK, RET_HEADS * RET_DK, RET_HEADS * RET_DV, RET_HEADS * RET_DV,
    ML_HEADS * ML_DK, ML_HEADS * ML_DK, ML_HEADS * ML_DV, ML_HEADS * ML_DV, ML_HEADS, ML_HEADS,
)
D_IN = sum(IN_SIZES)

XA_HEADS = 4
XA_DH = D_MODEL // XA_HEADS
D_FF = ((8 * D_MODEL // 3 + 127) // 128) * 128
MAX_POS_OFFSET = 1024

kernel_name = "hybrid_gla_retnet_mlstm_macaron"


def rmsnorm(x, g):
    xf = x.astype(F32)
    y = xf * lax.rsqrt(jnp.mean(xf * xf, axis=-1, keepdims=True) + EPS)
    return (y * g.astype(F32)).astype(x.dtype)


def head_rmsnorm(x, g):
    b, s, h, d = x.shape
    xf = x.astype(F32)
    y = xf * lax.rsqrt(jnp.mean(xf * xf, axis=-1, keepdims=True) + EPS)
    return y.reshape(b, s, h * d) * g.astype(F32)


def head_groupnorm(x, g):
    b, s, h, d = x.shape
    xf = x.astype(F32)
    mu = jnp.mean(xf, axis=-1, keepdims=True)
    xc = xf - mu
    y = xc * lax.rsqrt(jnp.mean(xc * xc, axis=-1, keepdims=True) + EPS)
    return y.reshape(b, s, h * d) * g.astype(F32)


def swiglu_ffn(x, w_gu, w_down):
    a, g = jnp.split(x @ w_gu, 2, axis=-1)
    return (jax.nn.silu(a) * g) @ w_down


def _to_chunks(t, c):
    b, s, h, d = t.shape
    return t.reshape(b, s // c, c, h, d).transpose(1, 0, 3, 2, 4)


def _from_chunks(t):
    n, b, h, c, d = t.shape
    return t.transpose(1, 0, 3, 2, 4).reshape(b, n * c, h, d)


def rotary(x, positions):
    half = x.shape[-1] // 2
    inv_freq = 1.0 / (ROPE_BASE ** jnp.linspace(0.0, 1.0, half, dtype=F32))
    ang = positions.astype(F32)[..., None] * inv_freq
    cos = jnp.cos(ang)[:, :, None, :]
    sin = jnp.sin(ang)[:, :, None, :]
    x1 = x[..., :half].astype(F32)
    x2 = x[..., half:].astype(F32)
    return jnp.concatenate([x1 * cos - x2 * sin, x1 * sin + x2 * cos], axis=-1).astype(x.dtype)


def causal_dwconv(x, w):
    k = w.shape[0]
    return lax.conv_general_dilated(
        x, w[:, None, :].astype(x.dtype), window_strides=(1,), padding=[(k - 1, 0)],
        dimension_numbers=("NWC", "WIO", "NWC"), feature_group_count=x.shape[-1])


def gla_chunked(q, k, v, log_a):
    b, s, h, dk = q.shape
    dv = v.shape[-1]
    c = GLA_CHUNK
    qc = _to_chunks(q * dk ** -0.5, c)
    kc = _to_chunks(k, c)
    vc = _to_chunks(v, c)
    ac = _to_chunks(log_a.astype(F32), c)
    causal = jnp.tril(jnp.ones((c, c), bool))[:, :, None]

    def step(state, inp):
        qi, ki, vi, ai = inp
        bcum = jnp.cumsum(ai, axis=2)
        b_last = bcum[:, :, -1:, :]
        diff = bcum[:, :, :, None, :] - bcum[:, :, None, :, :]
        decay = jnp.exp(jnp.where(causal, diff, -jnp.inf))
        scores = jnp.einsum("bhid,bhjd,bhijd->bhij", qi, ki, decay)
        o = (jnp.einsum("bhij,bhjv->bhiv", scores, vi)
             + jnp.einsum("bhid,bhdv->bhiv", qi * jnp.exp(bcum), state))
        state = (jnp.exp(b_last)[:, :, 0, :, None] * state
                 + jnp.einsum("bhjd,bhjv->bhdv", ki * jnp.exp(b_last - bcum), vi))
        return state, o

    s0 = jnp.zeros((b, h, dk, dv), F32)
    _, out = lax.scan(step, s0, (qc, kc, vc, ac))
    return _from_chunks(out)


def retention_chunked(q, k, v):
    b, s, h, dk = q.shape
    dv = v.shape[-1]
    c = RET_CHUNK
    log_g = jnp.log(1.0 - 2.0 ** (-5.0 - jnp.arange(h, dtype=F32)))
    idx = jnp.arange(c, dtype=F32)
    rel = idx[:, None] - idx[None, :]
    decay = jnp.where(rel >= 0, jnp.exp(log_g[:, None, None] * jnp.maximum(rel, 0.0)), 0.0)
    q_dec = jnp.exp(log_g[:, None] * (idx + 1.0))[:, :, None]
    k_dec = jnp.exp(log_g[:, None] * (c - 1.0 - idx))[:, :, None]
    chunk_dec = jnp.exp(log_g * c)[:, None, None]
    qc = _to_chunks(q, c)
    kc = _to_chunks(k * dk ** -0.5, c)
    vc = _to_chunks(v, c)

    def step(r, inp):
        qi, ki, vi = inp
        scores = jnp.einsum("bhid,bhjd->bhij", qi, ki) * decay
        o = jnp.einsum("bhij,bhjv->bhiv", scores, vi) + q_dec * jnp.einsum("bhid,bhdv->bhiv", qi, r)
        r = chunk_dec * r + jnp.einsum("bhjd,bhjv->bhdv", ki * k_dec, vi)
        return r, o

    r0 = jnp.zeros((b, h, dk, dv), F32)
    _, out = lax.scan(step, r0, (qc, kc, vc))
    return _from_chunks(out)


def mlstm_chunked(q, k, v, i_pre, f_pre):
    b, s, h, dk = q.shape
    dv = v.shape[-1]
    c = ML_CHUNK
    qc = _to_chunks(q, c)
    kc = _to_chunks(k * dk ** -0.5, c)
    vc = _to_chunks(v, c)
    ic = _to_chunks(i_pre.astype(F32)[..., None], c)[..., 0]
    fc = _to_chunks(jax.nn.log_sigmoid(f_pre.astype(F32))[..., None], c)[..., 0]
    causal = jnp.tril(jnp.ones((c, c), bool))

    def step(carry, inp):
        c_st, n_st, m_st = carry
        qi, ki, vi, ii, lfi = inp
        bcum = jnp.cumsum(lfi, axis=-1)
        d_log = jnp.where(causal, bcum[..., :, None] - bcum[..., None, :] + ii[..., None, :], -jnp.inf)
        inter_log = bcum + m_st[..., None]
        m_t = jnp.maximum(inter_log, jnp.max(d_log, axis=-1))
        w_inter = jnp.exp(inter_log - m_t)
        s_qk = jnp.einsum("bhid,bhjd->bhij", qi, ki) * jnp.exp(d_log - m_t[..., None])
        num = (jnp.einsum("bhij,bhjv->bhiv", s_qk, vi)
               + w_inter[..., None] * jnp.einsum("bhid,bhdv->bhiv", qi, c_st))
        den = jnp.sum(s_qk, axis=-1) + w_inter * jnp.einsum("bhid,bhd->bhi", qi, n_st)
        h_t = num / jnp.maximum(jnp.abs(den), jnp.exp(-m_t))[..., None]
        b_last = bcum[..., -1]
        log_w = b_last[..., None] - bcum + ii
        m_new = jnp.maximum(b_last + m_st, jnp.max(log_w, axis=-1))
        g_prev = jnp.exp(b_last + m_st - m_new)
        wk = jnp.exp(log_w - m_new[..., None])[..., None] * ki
        c_new = g_prev[..., None, None] * c_st + jnp.einsum("bhjd,bhjv->bhdv", wk, vi)
        n_new = g_prev[..., None] * n_st + jnp.sum(wk, axis=2)
        return (c_new, n_new, m_new), h_t

    carry0 = (jnp.zeros((b, h, dk, dv), F32), jnp.zeros((b, h, dk), F32), jnp.zeros((b, h), F32))
    _, out = lax.scan(step, carry0, (qc, kc, vc, ic, fc))
    return _from_chunks(out)


def hybrid_mix(hn, positions, w_in, gla_w_a2, gla_b_a, gla_g_norm, ret_g_norm,
               ml_conv, ml_b_i, ml_b_f, w_out):
    b, s, _ = hn.shape
    split_points = np.cumsum(IN_SIZES)[:-1].tolist()
    (gq, gk, gv, ga, gr, rq, rk, rv, rg, mq, mk, mv, mo, mi, mf) = jnp.split(hn @ w_in, split_points, axis=-1)

    def heads(t, nh):
        return t.reshape(b, s, nh, -1)

    log_a = jax.nn.log_sigmoid((ga @ gla_w_a2 + gla_b_a).astype(F32)) / GLA_TAU
    o_gla = gla_chunked(heads(gq, GLA_HEADS), heads(gk, GLA_HEADS), heads(gv, GLA_HEADS), heads(log_a, GLA_HEADS))
    o_gla = (head_rmsnorm(o_gla, gla_g_norm) * jax.nn.silu(gr.astype(F32))).astype(hn.dtype)

    rq = rotary(heads(rq, RET_HEADS), positions)
    rk = rotary(heads(rk, RET_HEADS), positions)
    o_ret = retention_chunked(rq, rk, heads(rv, RET_HEADS))
    o_ret = (head_groupnorm(o_ret, ret_g_norm) * jax.nn.silu(rg.astype(F32))).astype(hn.dtype)

    mqk = jax.nn.silu(causal_dwconv(jnp.concatenate([mq, mk], axis=-1), ml_conv))
    mq, mk = jnp.split(mqk, 2, axis=-1)
    o_ml = mlstm_chunked(heads(mq, ML_HEADS), heads(mk, ML_HEADS), heads(mv, ML_HEADS), mi + ml_b_i, mf + ml_b_f)
    o_ml = (jax.nn.sigmoid(mo.astype(F32)) * o_ml.reshape(b, s, -1)).astype(hn.dtype)

    return jnp.concatenate([o_gla, o_ret, o_ml], axis=-1) @ w_out


def mem_cross_attn(hn, memn, w_q, w_kv, w_o):
    b, s, d = hn.shape
    m = memn.shape[1]
    q = (hn @ w_q).reshape(b, s, XA_HEADS, XA_DH)
    k, v = jnp.split(memn @ w_kv, 2, axis=-1)
    k = k.reshape(b, m, XA_HEADS, XA_DH)
    v = v.reshape(b, m, XA_HEADS, XA_DH)
    scores = jnp.einsum("bshd,bmhd->bhsm", q, k).astype(F32) * XA_DH ** -0.5
    p = jax.nn.softmax(scores, axis=-1).astype(v.dtype)
    o = jnp.einsum("bhsm,bmhd->bshd", p, v).reshape(b, s, d)
    return o @ w_o


def setup_inputs(seed: int = 0) -> dict:
    key = jax.random.key(seed)
    ks = jax.random.split(key, 32)
    L, D = DEPTH, D_MODEL

    def w(k, shape, fan_in):
        return jax.random.normal(k, shape, F32) * fan_in ** -0.5

    def gain(k, shape):
        return 1.0 + 0.02 * jax.random.normal(k, shape, F32)

    x = jax.random.normal(ks[0], (BATCH, SEQ, D), F32)
    mem = jax.random.normal(ks[1], (BATCH, MEM_LEN, D), F32)
    positions = (jnp.arange(SEQ, dtype=jnp.int32)[None, :]
                 + jax.random.randint(ks[2], (BATCH, 1), 0, MAX_POS_OFFSET, dtype=jnp.int32))
    ml_b_f = (jnp.linspace(ML_FBIAS_LO, ML_FBIAS_HI, ML_HEADS, dtype=F32)[None, :]
              + 0.1 * jax.random.normal(ks[14], (L, ML_HEADS), F32))
    return {
        "x": x,
        "mem": mem,
        "positions": positions,
        "g_ffa": gain(ks[3], (L, D)),
        "w_ffa_gu": w(ks[4], (L, D, 2 * D_FF), D),
        "w_ffa_down": w(ks[5], (L, D_FF, D), D_FF),
        "g_mix": gain(ks[6], (L, D)),
        "w_in": w(ks[7], (L, D, D_IN), D),
        "gla_w_a2": w(ks[8], (L, GLA_RANK, GLA_HEADS * GLA_DK), GLA_RANK),
        "gla_b_a": 0.1 * jax.random.normal(ks[9], (L, GLA_HEADS * GLA_DK), F32),
        "gla_g_norm": gain(ks[10], (L, GLA_HEADS * GLA_DV)),
        "ret_g_norm": gain(ks[11], (L, RET_HEADS * RET_DV)),
        "ml_conv": w(ks[12], (L, ML_CONV, 2 * ML_HEADS * ML_DK), ML_CONV),
        "ml_b_i": 0.1 * jax.random.normal(ks[13], (L, ML_HEADS), F32),
        "ml_b_f": ml_b_f,
        "w_out": w(ks[15], (L, D_MIX, D), D_MIX),
        "g_xa": gain(ks[16], (L, D)),
        "g_mem": gain(ks[17], (L, D)),
        "w_xa_q": w(ks[18], (L, D, D), D),
        "w_xa_kv": w(ks[19], (L, D, 2 * D), D),
        "w_xa_o": w(ks[20], (L, D, D), D),
        "g_ffb": gain(ks[21], (L, D)),
        "w_ffb_gu": w(ks[22], (L, D, 2 * D_FF), D),
        "w_ffb_down": w(ks[23], (L, D_FF, D), D_FF),
        "g_final": gain(ks[24], (D,)),
    }


def reference(x, mem, positions, g_ffa, w_ffa_gu, w_ffa_down, g_mix, w_in, gla_w_a2, gla_b_a,
              gla_g_norm, ret_g_norm, ml_conv, ml_b_i, ml_b_f, w_out, g_xa, g_mem, w_xa_q,
              w_xa_kv, w_xa_o, g_ffb, w_ffb_gu, w_ffb_down, g_final):
    for l in range(DEPTH):
        x = x + 0.5 * swiglu_ffn(rmsnorm(x, g_ffa[l]), w_ffa_gu[l], w_ffa_down[l])
        x = x + hybrid_mix(rmsnorm(x, g_mix[l]), positions, w_in[l], gla_w_a2[l], gla_b_a[l],
                           gla_g_norm[l], ret_g_norm[l], ml_conv[l], ml_b_i[l], ml_b_f[l], w_out[l])
        x = x + mem_cross_attn(rmsnorm(x, g_xa[l]), rmsnorm(mem, g_mem[l]), w_xa_q[l], w_xa_kv[l], w_xa_o[l])
        x = x + 0.5 * swiglu_ffn(rmsnorm(x, g_ffb[l]), w_ffb_gu[l], w_ffb_down[l])
    return rmsnorm(x, g_final)
```

```python
import functools

import numpy as np
import jax
import jax.numpy as jnp
from jax import lax
from jax.experimental import pallas as pl
from jax.experimental.pallas import tpu as pltpu

F32 = jnp.float32
BF16 = jnp.bfloat16
EPS = 1e-6

D_MODEL = 1024
GLA_HEADS, GLA_DK, GLA_DV, GLA_RANK, GLA_TAU = 6, 32, 64, 16, 16.0
RET_HEADS, RET_DK, RET_DV = 6, 32, 64
ML_HEADS, ML_DK, ML_DV, ML_CONV = 4, 64, 64, 4
ROPE_BASE = 10000.0
XA_HEADS = 4
XA_DH = D_MODEL // XA_HEADS
IN_SIZES = (
    GLA_HEADS * GLA_DK, GLA_HEADS * GLA_DK, GLA_HEADS * GLA_DV, GLA_RANK, GLA_HEADS * GLA_DV,
    RET_HEADS * RET_DK, RET_HEADS * RET_DK, RET_HEADS * RET_DV, RET_HEADS * RET_DV,
    ML_HEADS * ML_DK, ML_HEADS * ML_DK, ML_HEADS * ML_DV, ML_HEADS * ML_DV, ML_HEADS, ML_HEADS,
)

QK = GLA_HEADS * GLA_DK
VW = GLA_HEADS * GLA_DV
MLW = ML_HEADS * ML_DK
SMALL_W = 128
SM_I = GLA_RANK
SM_F = GLA_RANK + ML_HEADS

OFF_GQ, OFF_GK, OFF_GV, OFF_GR = 0, 256, 512, 896
OFF_RQ, OFF_RQS, OFF_RK, OFF_RKS, OFF_RV, OFF_RG = 1280, 1536, 1792, 2048, 2304, 2688
OFF_MQK, OFF_MV, OFF_MO, OFF_SM = 3072, 3584, 3840, 4096
W_CAT = 4224

MIX_CHUNK = 128
GLA_SAFE_LOG = 40.0
VMEM_LIMIT = 48 * 1024 * 1024


def _dot(a, b):
    return jnp.dot(a, b, preferred_element_type=F32)


def _dot_nt(a, b):
    return lax.dot_general(a, b, (((1,), (1,)), ((), ())), preferred_element_type=F32)


def _dot_tn(a, b):
    return lax.dot_general(a, b, (((0,), (0,)), ((), ())), preferred_element_type=F32)


def _sigmoid(x):
    return 1.0 / (1.0 + jnp.exp(-x))


def _silu(x):
    return x * _sigmoid(x)


def _log_sigmoid(x):
    return jnp.minimum(x, 0.0) - jnp.log1p(jnp.exp(-jnp.abs(x)))


def _rms(x, g):
    return x * lax.rsqrt(jnp.mean(x * x, axis=-1, keepdims=True) + EPS) * g


def _cumsum_rows(tri_bf, x):
    hi = x.astype(BF16)
    r1 = x - hi.astype(F32)
    mid = r1.astype(BF16)
    lo = (r1 - mid.astype(F32)).astype(BF16)
    return _dot(tri_bf, hi) + _dot(tri_bf, mid) + _dot(tri_bf, lo)


def _iota(shape, dim):
    return lax.broadcasted_iota(jnp.int32, shape, dim)


def _head_masks(width, head_w, n_heads):
    lane = _iota((1, width), 1)
    return [(lane >= h * head_w) & (lane < (h + 1) * head_w) for h in range(n_heads)]


def _params(sem):
    return pltpu.CompilerParams(dimension_semantics=sem, vmem_limit_bytes=VMEM_LIMIT)


def _ffn_kernel(x_ref, g_ref, wa_ref, wg_ref, wd_ref, gf_ref, o_ref, hn_ref, acc_ref, *, final):
    k = pl.program_id(1)

    @pl.when(k == 0)
    def _():
        hn_ref[...] = _rms(x_ref[...], g_ref[...]).astype(BF16)
        acc_ref[...] = jnp.zeros_like(acc_ref)

    hn = hn_ref[...]
    a = _dot(hn, wa_ref[...])
    g = _dot(hn, wg_ref[...])
    h = (_silu(a) * g).astype(BF16)
    acc_ref[...] += _dot(h, wd_ref[...])

    @pl.when(k == pl.num_programs(1) - 1)
    def _():
        y = x_ref[...] + 0.5 * acc_ref[...]
        if final:
            y = _rms(y, gf_ref[...])
        o_ref[...] = y


def _ffn(x2d, g, w_gu, w_down, g_final, *, final, tm=512, tf=1408):
    n, d = x2d.shape
    d_ff = w_down.shape[0]
    nf = d_ff // tf
    return pl.pallas_call(
        functools.partial(_ffn_kernel, final=final),
        out_shape=jax.ShapeDtypeStruct((n, d), F32),
        grid=(n // tm, nf),
        in_specs=[
            pl.BlockSpec((tm, d), lambda i, k: (i, 0)),
            pl.BlockSpec((1, d), lambda i, k: (0, 0)),
            pl.BlockSpec((d, tf), lambda i, k: (0, k)),
            pl.BlockSpec((d, tf), lambda i, k: (0, k + nf)),
            pl.BlockSpec((tf, d), lambda i, k: (k, 0)),
            pl.BlockSpec((1, d), lambda i, k: (0, 0)),
        ],
        out_specs=pl.BlockSpec((tm, d), lambda i, k: (i, 0)),
        scratch_shapes=[pltpu.VMEM((tm, d), BF16), pltpu.VMEM((tm, d), F32)],
        compiler_params=_params(("parallel", "arbitrary")),
        name="ffn",
    )(x2d, g, w_gu, w_gu, w_down, g_final)


def _rope_kernel(pos_ref, invf_ref, sgn_ref, cos_ref, sin_ref):
    ang = pos_ref[...].astype(F32) * invf_ref[...]
    cos_ref[...] = jnp.cos(ang)
    sin_ref[...] = jnp.sin(ang) * sgn_ref[...]


def _rope_tables(pos2d, *, tm=512):
    n = pos2d.shape[0]
    half = RET_DK // 2
    inv_freq = 1.0 / (ROPE_BASE ** jnp.linspace(0.0, 1.0, half, dtype=F32))
    invf = jnp.tile(jnp.concatenate([inv_freq, inv_freq]), RET_HEADS)[None, :]
    sgn = jnp.tile(jnp.concatenate([-jnp.ones((half,), F32), jnp.ones((half,), F32)]), RET_HEADS)[None, :]
    return pl.pallas_call(
        _rope_kernel,
        out_shape=(jax.ShapeDtypeStruct((n, QK), F32), jax.ShapeDtypeStruct((n, QK), F32)),
        grid=(n // tm,),
        in_specs=[
            pl.BlockSpec((tm, 1), lambda i: (i, 0)),
            pl.BlockSpec((1, QK), lambda i: (0, 0)),
            pl.BlockSpec((1, QK), lambda i: (0, 0)),
        ],
        out_specs=(pl.BlockSpec((tm, QK), lambda i: (i, 0)), pl.BlockSpec((tm, QK), lambda i: (i, 0))),
        compiler_params=_params(("parallel",)),
        name="rope_tables",
    )(pos2d, invf, sgn)


def _inproj_kernel(x_ref, g_ref, w_ref, cos_ref, sin_ref,
                   gq_ref, gk_ref, gv_ref, gr_ref, rq_ref, rk_ref, rv_ref, rg_ref,
                   mqk_ref, mv_ref, mo_ref, sm_ref):
    hn = _rms(x_ref[...], g_ref[...]).astype(BF16)

    def seg(off, width):
        return _dot(hn, w_ref[:, off:off + width])

    gq_ref[...] = seg(OFF_GQ, QK)
    gk_ref[...] = seg(OFF_GK, QK)
    gv_ref[...] = seg(OFF_GV, VW).astype(BF16)
    gr_ref[...] = seg(OFF_GR, VW)
    c = cos_ref[...]
    s = sin_ref[...]
    rq_ref[...] = seg(OFF_RQ, QK) * c + seg(OFF_RQS, QK) * s
    rk_ref[...] = seg(OFF_RK, QK) * c + seg(OFF_RKS, QK) * s
    rv_ref[...] = seg(OFF_RV, VW).astype(BF16)
    rg_ref[...] = seg(OFF_RG, VW)
    mqk_ref[...] = seg(OFF_MQK, 2 * MLW)
    mv_ref[...] = seg(OFF_MV, MLW).astype(BF16)
    mo_ref[...] = seg(OFF_MO, MLW)
    sm_ref[...] = seg(OFF_SM, SMALL_W)


def _inproj(x2d, g, w_cat, cos_t, sin_t, *, tm=512):
    n, d = x2d.shape
    widths = [(QK, F32), (QK, F32), (VW, BF16), (VW, F32), (QK, F32), (QK, F32), (VW, BF16), (VW, F32),
              (2 * MLW, F32), (MLW, BF16), (MLW, F32), (SMALL_W, F32)]
    row = lambda w: pl.BlockSpec((tm, w), lambda i: (i, 0))
    return pl.pallas_call(
        _inproj_kernel,
        out_shape=tuple(jax.ShapeDtypeStruct((n, w), dt) for w, dt in widths),
        grid=(n // tm,),
        in_specs=[row(d), pl.BlockSpec((1, d), lambda i: (0, 0)),
                  pl.BlockSpec((d, W_CAT), lambda i: (0, 0)), row(QK), row(QK)],
        out_specs=tuple(row(w) for w, _ in widths),
        compiler_params=_params(("parallel",)),
        name="inproj",
    )(x2d, g, w_cat, cos_t, sin_t)


def _build_w_cat(w_in):
    L, d, _ = w_in.shape
    offs = np.concatenate([[0], np.cumsum(IN_SIZES)])
    col = lambda i: w_in[:, :, offs[i]:offs[i + 1]]
    (gq, gk, gv, ga, gr, rq, rk, rv, rg, mq, mk, mv, mo, mi, mf) = [col(i) for i in range(15)]

    def swap(t):
        half = RET_DK // 2
        return t.reshape(L, d, RET_HEADS, 2, half)[:, :, :, ::-1, :].reshape(L, d, QK)

    def pad(t, width):
        return jnp.pad(t, ((0, 0), (0, 0), (0, width - t.shape[-1])))

    small = jnp.concatenate([ga, mi, mf], axis=-1)
    pieces = [pad(gq, 256), pad(gk, 256), gv, gr, pad(rq, 256), pad(swap(rq), 256), pad(rk, 256),
              pad(swap(rk), 256), rv, rg, mq, mk, mv, mo, pad(small, SMALL_W)]
    w_cat = jnp.concatenate(pieces, axis=-1).astype(BF16)
    assert w_cat.shape[-1] == W_CAT
    return w_cat


def _pair_sum(x, lo):
    s_lo = jnp.sum(jnp.where(lo, x, 0.0), axis=-1, keepdims=True)
    s_hi = jnp.sum(jnp.where(lo, 0.0, x), axis=-1, keepdims=True)
    return jnp.where(lo, s_lo, s_hi)


def _intra_pairs(scores_of_head, v_ref, rows, n_heads, lo):
    outs = []
    for p in range(n_heads // 2):
        vp = v_ref[rows, 128 * p:128 * (p + 1)]
        zero = jnp.zeros_like(vp)
        acc = _dot(scores_of_head(2 * p), jnp.where(lo, vp, zero))
        acc = acc + _dot(scores_of_head(2 * p + 1), jnp.where(lo, zero, vp))
        outs.append(acc)
    return outs


def _gla_kernel(q_ref, k_ref, sm_ref, v_ref, r_ref, wa2_ref, ba_ref, gn_ref, o_ref,
                st_ref, la_ref, oacc_ref, *, T, C):
    @pl.when(pl.program_id(1) == 0)
    def _():
        st_ref[...] = jnp.zeros_like(st_ref)

    hmask = _head_masks(QK, GLA_DK, GLA_HEADS)
    lo = _iota((1, 128), 1) < GLA_DV
    causal = _iota((C, C), 0) >= _iota((C, C), 1)
    tri = causal.astype(BF16)
    bd = (_iota((VW, QK), 0) // GLA_DV) == (_iota((VW, QK), 1) // GLA_DK)
    scale = GLA_DK ** -0.5

    def chunk(ci, carry):
        rows = pl.ds(pl.multiple_of(ci * C, C), C)
        z = _dot(sm_ref[rows, :].astype(BF16), wa2_ref[...]) + ba_ref[...]
        la = _log_sigmoid(z) * (1.0 / GLA_TAU)
        bc = _cumsum_rows(tri, la)
        bl = bc[C - 1:C, :]
        safe = jnp.min(bl) > -GLA_SAFE_LOG

        @pl.when(safe)
        def _():
            q = q_ref[rows, :]
            k = k_ref[rows, :]
            qt = q * (scale * jnp.exp(bc))
            kt = (k * jnp.exp(-bc)).astype(BF16)
            kh = (k * jnp.exp(bl - bc)).astype(BF16)

            def scores(h):
                s = _dot_nt(jnp.where(hmask[h], qt, 0.0).astype(BF16), kt)
                return jnp.where(causal, s, 0.0).astype(BF16)

            intra = _intra_pairs(scores, v_ref, rows, GLA_HEADS, lo)
            st = st_ref[...]
            inter = _dot_nt(qt.astype(BF16), st.astype(BF16))
            for p in range(GLA_HEADS // 2):
                oacc_ref[:, 128 * p:128 * (p + 1)] = intra[p] + inter[:, 128 * p:128 * (p + 1)]
            upd = _dot_tn(v_ref[rows, :], kh)
            st_ref[...] = st * jnp.exp(bl) + jnp.where(bd, upd, 0.0)

        @pl.when(jnp.logical_not(safe))
        def _():
            la_ref[...] = la
            vb = v_ref[rows, :]
            row_id = _iota((C, 2 * 128), 0)

            def token(t, c2):
                a_t = jnp.exp(la_ref[pl.ds(t, 1), :])
                k_t = k_ref[pl.ds(ci * C + t, 1), :]
                q_t = q_ref[pl.ds(ci * C + t, 1), :] * scale
                onehot = (row_id == t).astype(BF16)
                v_col = _dot_tn(vb, onehot)[:, :QK]
                st = st_ref[...] * a_t + jnp.where(bd, v_col * k_t.astype(BF16).astype(F32), 0.0)
                st_ref[...] = st
                q8 = jnp.broadcast_to(q_t, (8, QK)).astype(BF16)
                oacc_ref[pl.ds(t, 1), :] = _dot_nt(q8, st.astype(BF16))[0:1, :]
                return c2

            lax.fori_loop(0, C, token, 0)

        for p in range(GLA_HEADS // 2):
            cols = slice(128 * p, 128 * (p + 1))
            o = oacc_ref[:, cols]
            ms = _pair_sum(o * o, lo) * (1.0 / GLA_DV)
            y = o * lax.rsqrt(ms + EPS) * gn_ref[:, cols] * _silu(r_ref[rows, cols])
            o_ref[rows, cols] = y.astype(BF16)
        return carry

    lax.fori_loop(0, T // C, chunk, 0)


def _gla(q, k, sm, v, r, wa2p, ba, gn, *, B, S, T=512, C=MIX_CHUNK):
    nt = S // T
    row = lambda w: pl.BlockSpec((T, w), lambda b, i: (b * nt + i, 0))
    full = lambda a, c: pl.BlockSpec((a, c), lambda b, i: (0, 0))
    return pl.pallas_call(
        functools.partial(_gla_kernel, T=T, C=C),
        out_shape=jax.ShapeDtypeStruct((B * S, VW), BF16),
        grid=(B, nt),
        in_specs=[row(QK), row(QK), row(SMALL_W), row(VW), row(VW),
                  full(SMALL_W, QK), full(1, QK), full(1, VW)],
        out_specs=row(VW),
        scratch_shapes=[pltpu.VMEM((VW, QK), F32), pltpu.VMEM((C, QK), F32), pltpu.VMEM((C, VW), F32)],
        compiler_params=_params(("arbitrary", "arbitrary")),
        name="gla",
    )(q, k, sm, v, r, wa2p, ba, gn)


def _ret_kernel(q_ref, k_ref, v_ref, rg_ref, gn_ref, o_ref, rt_ref, *, T, C):
    @pl.when(pl.program_id(1) == 0)
    def _():
        rt_ref[...] = jnp.zeros_like(rt_ref)

    hmask = _head_masks(QK, RET_DK, RET_HEADS)
    lo = _iota((1, 128), 1) < RET_DV
    log_g = [float(np.log(1.0 - 2.0 ** (-5.0 - h))) for h in range(RET_HEADS)]
    lane_k = _iota((1, QK), 1) // RET_DK
    lane_v = _iota((1, VW), 1) // RET_DV
    lg_k = jnp.zeros((1, QK), F32)
    lg_v = jnp.zeros((1, VW), F32)
    for h in range(RET_HEADS):
        lg_k = jnp.where(lane_k == h, log_g[h], lg_k)
        lg_v = jnp.where(lane_v == h, log_g[h], lg_v)
    rel = (_iota((C, C), 0) - _iota((C, C), 1)).astype(F32)
    causal = rel >= 0.0
    idx = _iota((C, 1), 0).astype(F32)
    k_dec = jnp.exp(lg_k * (C - 1.0 - idx)) * (RET_DK ** -0.5)
    q_dec = jnp.exp(lg_v * (idx + 1.0))
    chunk_dec = jnp.exp(lg_k * float(C))
    bd = (_iota((VW, QK), 0) // RET_DV) == (_iota((VW, QK), 1) // RET_DK)
    k_scale = RET_DK ** -0.5

    def chunk(ci, carry):
        rows = pl.ds(pl.multiple_of(ci * C, C), C)
        q = q_ref[rows, :]
        k = k_ref[rows, :]
        ks = (k * k_scale).astype(BF16)

        def scores(h):
            s = _dot_nt(jnp.where(hmask[h], q, 0.0).astype(BF16), ks)
            decay = jnp.where(causal, jnp.exp(log_g[h] * jnp.maximum(rel, 0.0)), 0.0)
            return (s * decay).astype(BF16)

        intra = _intra_pairs(scores, v_ref, rows, RET_HEADS, lo)
        rt = rt_ref[...]
        inter = _dot_nt(q.astype(BF16), rt.astype(BF16)) * q_dec
        upd = _dot_tn(v_ref[rows, :], (k * k_dec).astype(BF16))
        rt_ref[...] = rt * chunk_dec + jnp.where(bd, upd, 0.0)
        for p in range(RET_HEADS // 2):
            cols = slice(128 * p, 128 * (p + 1))
            o = intra[p] + inter[:, cols]
            mu = _pair_sum(o, lo) * (1.0 / RET_DV)
            xc = o - mu
            var = _pair_sum(xc * xc, lo) * (1.0 / RET_DV)
            y = xc * lax.rsqrt(var + EPS) * gn_ref[:, cols] * _silu(rg_ref[rows, cols])
            o_ref[rows, cols] = y.astype(BF16)
        return carry

    lax.fori_loop(0, T // C, chunk, 0)


def _ret(q, k, v, rg, gn, *, B, S, T=512, C=MIX_CHUNK):
    nt = S // T
    row = lambda w: pl.BlockSpec((T, w), lambda b, i: (b * nt + i, 0))
    return pl.pallas_call(
        functools.partial(_ret_kernel, T=T, C=C),
        out_shape=jax.ShapeDtypeStruct((B * S, VW), BF16),
        grid=(B, nt),
        in_specs=[row(QK), row(QK), row(VW), row(VW), pl.BlockSpec((1, VW), lambda b, i: (0, 0))],
        out_specs=row(VW),
        scratch_shapes=[pltpu.VMEM((VW, QK), F32)],
        compiler_params=_params(("arbitrary", "arbitrary")),
        name="retention",
    )(q, k, v, rg, gn)


def _mlstm_kernel(qk_ref, cw_ref, sm_ref, bias_ref, v_ref, og_ref, o_ref,
                  ct_ref, n_ref, m_ref, cv_ref, q_s, k_s, *, T, C):
    @pl.when(pl.program_id(1) == 0)
    def _():
        ct_ref[...] = jnp.zeros_like(ct_ref)
        n_ref[...] = jnp.zeros_like(n_ref)
        m_ref[...] = jnp.zeros_like(m_ref)
        cv_ref[...] = jnp.zeros_like(cv_ref)

    x = qk_ref[...]
    w = cw_ref[...]
    prev = cv_ref[...]
    x8 = x[0:8, :]
    row8 = _iota((8, 2 * MLW), 0)
    acc = x * w[ML_CONV - 1:ML_CONV, :]
    acc8 = x8 * w[ML_CONV - 1:ML_CONV, :]
    for s in range(1, ML_CONV):
        tap = w[ML_CONV - 1 - s:ML_CONV - s, :]
        acc = acc + pltpu.roll(x, s, 0) * tap
        acc8 = acc8 + jnp.where(row8 < s, pltpu.roll(prev, s, 0), pltpu.roll(x8, s, 0)) * tap
    cv_ref[...] = x[T - 8:T, :]
    y = _silu(jnp.concatenate([acc8, acc[8:, :]], axis=0))
    q_s[...] = y[:, :MLW]
    k_s[...] = y[:, MLW:] * (ML_DK ** -0.5)

    hmask = _head_masks(MLW, ML_DK, ML_HEADS)
    lo = _iota((1, 128), 1) < ML_DV
    causal = _iota((C, C), 0) >= _iota((C, C), 1)
    tri = causal.astype(BF16)
    bd = (_iota((MLW, MLW), 0) // ML_DV) == (_iota((MLW, MLW), 1) // ML_DK)
    lane_m = _iota((1, 128), 1)

    def chunk(ci, carry):
        rows = pl.ds(pl.multiple_of(ci * C, C), C)
        pre = sm_ref[rows, :] + bias_ref[...]
        bcum = _cumsum_rows(tri, _log_sigmoid(pre))
        bcum_t = bcum.T
        pre_t = pre.T
        q = q_s[rows, :]
        k = k_s[rows, :]
        kb = k.astype(BF16)
        n_prev = n_ref[...]
        m_all = m_ref[...]
        qn = q * n_prev

        s_list, win_list, den_list, mt_list = [], [], [], []
        wf = jnp.zeros((C, MLW), F32)
        gf = jnp.zeros((1, MLW), F32)
        m_next = m_all
        for h in range(ML_HEADS):
            bcol = bcum[:, SM_F + h:SM_F + h + 1]
            brow = bcum_t[SM_F + h:SM_F + h + 1, :]
            icol = pre[:, SM_I + h:SM_I + h + 1]
            irow = pre_t[SM_I + h:SM_I + h + 1, :]
            m_prev = m_all[:, h:h + 1]
            d_log = jnp.where(causal, bcol - brow + irow, -jnp.inf)
            inter_log = bcol + m_prev
            m_t = jnp.maximum(inter_log, jnp.max(d_log, axis=-1, keepdims=True))
            w_inter = jnp.exp(inter_log - m_t)
            s = _dot_nt(jnp.where(hmask[h], q, 0.0).astype(BF16), kb) * jnp.exp(d_log - m_t)
            den = (jnp.sum(s, axis=-1, keepdims=True)
                   + w_inter * jnp.sum(jnp.where(hmask[h], qn, 0.0), axis=-1, keepdims=True))
            s_list.append(s.astype(BF16))
            win_list.append(w_inter)
            den_list.append(jnp.maximum(jnp.abs(den), jnp.exp(-m_t)))
            b_last = bcol[C - 1:C, :]
            log_w = b_last - bcol + icol
            m_new = jnp.maximum(b_last + m_prev, jnp.max(log_w, axis=0, keepdims=True))
            g_prev = jnp.exp(b_last + m_prev - m_new)
            wf = wf + jnp.where(hmask[h], jnp.exp(log_w - m_new), 0.0)
            gf = gf + jnp.where(hmask[h], g_prev, 0.0)
            m_next = jnp.where(lane_m == h, m_new, m_next)

        intra = _intra_pairs(lambda h: s_list[h], v_ref, rows, ML_HEADS, lo)
        ct = ct_ref[...]
        inter = _dot_nt(q.astype(BF16), ct.astype(BF16))
        for p in range(ML_HEADS // 2):
            cols = slice(128 * p, 128 * (p + 1))
            w_i = jnp.where(lo, win_list[2 * p], win_list[2 * p + 1])
            dn = jnp.where(lo, den_list[2 * p], den_list[2 * p + 1])
            h_t = (intra[p] + w_i * inter[:, cols]) / dn
            o_ref[rows, cols] = (_sigmoid(og_ref[rows, cols]) * h_t).astype(BF16)

        wk = wf * k
        upd = _dot_tn(v_ref[rows, :], wk.astype(BF16))
        ct_ref[...] = ct * gf + jnp.where(bd, upd, 0.0)
        n_ref[...] = n_prev * gf + jnp.sum(wk, axis=0, keepdims=True)
        m_ref[...] = m_next
        return carry

    lax.fori_loop(0, T // C, chunk, 0)


def _mlstm(qk, cw, sm, bias, v, og, *, B, S, T=512, C=MIX_CHUNK):
    nt = S // T
    row = lambda w: pl.BlockSpec((T, w), lambda b, i: (b * nt + i, 0))
    full = lambda a, c: pl.BlockSpec((a, c), lambda b, i: (0, 0))
    return pl.pallas_call(
        functools.partial(_mlstm_kernel, T=T, C=C),
        out_shape=jax.ShapeDtypeStruct((B * S, MLW), BF16),
        grid=(B, nt),
        in_specs=[row(2 * MLW), full(ML_CONV, 2 * MLW), row(SMALL_W), full(1, SMALL_W), row(MLW), row(MLW)],
        out_specs=row(MLW),
        scratch_shapes=[pltpu.VMEM((MLW, MLW), F32), pltpu.VMEM((1, MLW), F32), pltpu.VMEM((1, 128), F32),
                        pltpu.VMEM((8, 2 * MLW), F32), pltpu.VMEM((T, MLW), F32), pltpu.VMEM((T, MLW), F32)],
        compiler_params=_params(("arbitrary", "arbitrary")),
        name="mlstm",
    )(qk, cw, sm, bias, v, og)


def _memkv_kernel(mem_ref, g_ref, w_ref, k_ref, v_ref):
    mn = _rms(mem_ref[0], g_ref[...]).astype(BF16)
    d = mem_ref.shape[-1]
    k_ref[0] = _dot(mn, w_ref[:, :d]).astype(BF16)
    v_ref[0] = _dot(mn, w_ref[:, d:]).astype(BF16)


def _memkv(mem, g, w_kv):
    b, m, d = mem.shape
    blk = pl.BlockSpec((1, m, d), lambda i: (i, 0, 0))
    return pl.pallas_call(
        _memkv_kernel,
        out_shape=(jax.ShapeDtypeStruct((b, m, d), BF16), jax.ShapeDtypeStruct((b, m, d), BF16)),
        grid=(b,),
        in_specs=[blk, pl.BlockSpec((1, d), lambda i: (0, 0)), pl.BlockSpec((d, 2 * d), lambda i: (0, 0))],
        out_specs=(blk, blk),
        compiler_params=_params(("parallel",)),
        name="mem_kv",
    )(mem, g, w_kv)


def _outxa_kernel(x_ref, og_ref, or_ref, om_ref, wo_ref, g_ref, wq_ref, k_ref, v_ref, wxo_ref, o_ref):
    x1 = (x_ref[...] + _dot(og_ref[...], wo_ref[0:VW, :]) + _dot(or_ref[...], wo_ref[VW:2 * VW, :])
          + _dot(om_ref[...], wo_ref[2 * VW:, :]))
    hn = _rms(x1, g_ref[...]).astype(BF16)
    q = (_dot(hn, wq_ref[...]) * (XA_DH ** -0.5)).astype(BF16)
    outs = []
    for h in range(XA_HEADS):
        cols = slice(XA_DH * h, XA_DH * (h + 1))
        s = _dot_nt(q[:, cols], k_ref[0, :, cols])
        s = s - jnp.max(s, axis=-1, keepdims=True)
        e = jnp.exp(s)
        p = e / jnp.sum(e, axis=-1, keepdims=True)
        outs.append(_dot(p.astype(BF16), v_ref[0, :, cols]).astype(BF16))
    o = jnp.concatenate(outs, axis=-1)
    o_ref[...] = x1 + _dot(o, wxo_ref[...])


def _outxa(x2d, o_gla, o_ret, o_ml, w_out, g_xa, w_q, mem_k, mem_v, w_o, *, B, S, tm=512):
    n, d = x2d.shape
    nt = S // tm
    m = mem_k.shape[1]
    row = lambda w: pl.BlockSpec((tm, w), lambda b, i: (b * nt + i, 0))
    full = lambda a, c: pl.BlockSpec((a, c), lambda b, i: (0, 0))
    kv = pl.BlockSpec((1, m, d), lambda b, i: (b, 0, 0))
    return pl.pallas_call(
        _outxa_kernel,
        out_shape=jax.ShapeDtypeStruct((n, d), F32),
        grid=(B, nt),
        in_specs=[row(d), row(VW), row(VW), row(MLW), full(d, d), full(1, d), full(d, d), kv, kv, full(d, d)],
        out_specs=row(d),
        compiler_params=_params(("parallel", "parallel")),
        name="outproj_xattn",
    )(x2d, o_gla, o_ret, o_ml, w_out, g_xa, w_q, mem_k, mem_v, w_o)


def kernel(x, mem, positions, g_ffa, w_ffa_gu, w_ffa_down, g_mix, w_in, gla_w_a2, gla_b_a, gla_g_norm,
           ret_g_norm, ml_conv, ml_b_i, ml_b_f, w_out, g_xa, g_mem, w_xa_q, w_xa_kv, w_xa_o, g_ffb,
           w_ffb_gu, w_ffb_down, g_final):
    B, S, D = x.shape
    L = w_in.shape[0]
    N = B * S
    bf = lambda t: t.astype(BF16)
    w_ffa_gu, w_ffa_down, w_ffb_gu, w_ffb_down = bf(w_ffa_gu), bf(w_ffa_down), bf(w_ffb_gu), bf(w_ffb_down)
    w_out_b, w_q_b, w_kv_b, w_o_b = bf(w_out), bf(w_xa_q), bf(w_xa_kv), bf(w_xa_o)
    w_cat = _build_w_cat(w_in)
    wa2p = bf(jnp.pad(gla_w_a2, ((0, 0), (0, SMALL_W - GLA_RANK), (0, 0))))
    ml_bias = jnp.pad(jnp.concatenate([ml_b_i, ml_b_f], axis=-1),
                      ((0, 0), (SM_I, SMALL_W - SM_I - 2 * ML_HEADS)))
    row = lambda t: t[None, :]

    cos_t, sin_t = _rope_tables(positions.reshape(N, 1))
    h = x.reshape(N, D)
    for l in range(L):
        h = _ffn(h, row(g_ffa[l]), w_ffa_gu[l], w_ffa_down[l], row(g_final), final=False)
        (gq, gk, gv, gr, rq, rk, rv, rg, mqk, mv, mo, sm) = _inproj(h, row(g_mix[l]), w_cat[l], cos_t, sin_t)
        o_gla = _gla(gq, gk, sm, gv, gr, wa2p[l], row(gla_b_a[l]), row(gla_g_norm[l]), B=B, S=S)
        o_ret = _ret(rq, rk, rv, rg, row(ret_g_norm[l]), B=B, S=S)
        o_ml = _mlstm(mqk, ml_conv[l], sm, row(ml_bias[l]), mv, mo, B=B, S=S)
        mem_k, mem_v = _memkv(mem, row(g_mem[l]), w_kv_b[l])
        h = _outxa(h, o_gla, o_ret, o_ml, w_out_b[l], row(g_xa[l]), w_q_b[l], mem_k, mem_v, w_o_b[l], B=B, S=S)
        h = _ffn(h, row(g_ffb[l]), w_ffb_gu[l], w_ffb_down[l], row(g_final), final=(l == L - 1))
    return h.reshape(B, S, D)
```

```python
import functools

import numpy as np
import jax
import jax.numpy as jnp
from jax import lax
from jax.experimental import pallas as pl
from jax.experimental.pallas import tpu as pltpu

F32 = jnp.float32
BF16 = jnp.bfloat16
EPS = 1e-6

D_MODEL = 1024
GLA_HEADS, GLA_DK, GLA_DV, GLA_RANK, GLA_TAU = 6, 32, 64, 16, 16.0
RET_HEADS, RET_DK, RET_DV = 6, 32, 64
ML_HEADS, ML_DK, ML_DV, ML_CONV = 4, 64, 64, 4
ROPE_BASE = 10000.0
XA_HEADS = 4
XA_DH = D_MODEL // XA_HEADS
IN_SIZES = (
    GLA_HEADS * GLA_DK, GLA_HEADS * GLA_DK, GLA_HEADS * GLA_DV, GLA_RANK, GLA_HEADS * GLA_DV,
    RET_HEADS * RET_DK, RET_HEADS * RET_DK, RET_HEADS * RET_DV, RET_HEADS * RET_DV,
    ML_HEADS * ML_DK, ML_HEADS * ML_DK, ML_HEADS * ML_DV, ML_HEADS * ML_DV, ML_HEADS, ML_HEADS,
)

LANES = 128
QK = GLA_HEADS * GLA_DK
VW = GLA_HEADS * GLA_DV
MLW = ML_HEADS * ML_DK
SMALL_W = LANES
SM_I = GLA_RANK
SM_F = GLA_RANK + ML_HEADS

OFF_GQ, OFF_GK, OFF_GV, OFF_GR = 0, 256, 512, 896
OFF_RQ, OFF_RQS, OFF_RK, OFF_RKS, OFF_RV, OFF_RG = 1280, 1536, 1792, 2048, 2304, 2688
OFF_MQK, OFF_MV, OFF_MO, OFF_SM = 3072, 3584, 3840, 4096
W_CAT = 4224

MIX_CHUNK = 128
MIX_TILE = 512
GLA_SAFE_LOG = 40.0
VMEM_LIMIT = 48 * 1024 * 1024


def _dot(a, b):
    return jnp.dot(a, b, preferred_element_type=F32)


def _dot_nt(a, b):
    return lax.dot_general(a, b, (((1,), (1,)), ((), ())), preferred_element_type=F32)


def _dot_tn(a, b):
    return lax.dot_general(a, b, (((0,), (0,)), ((), ())), preferred_element_type=F32)


def _sigmoid(x):
    return 1.0 / (1.0 + jnp.exp(-x))


def _silu(x):
    return x * _sigmoid(x)


def _log_sigmoid(x):
    return jnp.minimum(x, 0.0) - jnp.log1p(jnp.exp(-jnp.abs(x)))


def _rms(x, g):
    return x * lax.rsqrt(jnp.mean(x * x, axis=-1, keepdims=True) + EPS) * g


def _split2(x):
    hi = x.astype(BF16)
    return hi, (x - hi.astype(F32)).astype(BF16)


def _split3(x):
    hi = x.astype(BF16)
    r1 = x - hi.astype(F32)
    mid = r1.astype(BF16)
    return hi, mid, (r1 - mid.astype(F32)).astype(BF16)


def _cumsum_rows(tri_bf, x):
    hi, mid, lo = _split3(x)
    return _dot(tri_bf, hi) + _dot(tri_bf, mid) + _dot(tri_bf, lo)


def _iota(shape, dim):
    return lax.broadcasted_iota(jnp.int32, shape, dim)


def _lane_mask(width, lo, hi, dtype=BF16):
    lane = _iota((1, width), 1)
    return ((lane >= lo) & (lane < hi)).astype(dtype)


def _params(sem):
    return pltpu.CompilerParams(dimension_semantics=sem, vmem_limit_bytes=VMEM_LIMIT)


FFN_SUB = 256
FFN_CHUNK = 1024


def _ffn_kernel(x_ref, g_ref, wgu_ref, wd_ref, gf_ref, o_ref, *, final):
    tm = x_ref.shape[0]
    d_ff = wd_ref.shape[0]
    for r in range(tm // FFN_SUB):
        rows = slice(r * FFN_SUB, (r + 1) * FFN_SUB)
        x = x_ref[rows, :]
        hn = _rms(x, g_ref[...]).astype(BF16)
        acc = None
        for off in range(0, d_ff, FFN_CHUNK):
            cw = min(FFN_CHUNK, d_ff - off)
            a = _dot(hn, wgu_ref[:, off:off + cw])
            g = _dot(hn, wgu_ref[:, d_ff + off:d_ff + off + cw])
            part = _dot((_silu(a) * g).astype(BF16), wd_ref[off:off + cw, :])
            acc = part if acc is None else acc + part
        y = x + 0.5 * acc
        if final:
            y = _rms(y, gf_ref[...])
        o_ref[rows, :] = y


def _ffn(x2d, g, w_gu, w_down, g_final, *, final, tm=512):
    n, d = x2d.shape
    d_ff = w_down.shape[0]
    once = pl.Buffered(1)
    return pl.pallas_call(
        functools.partial(_ffn_kernel, final=final),
        out_shape=jax.ShapeDtypeStruct((n, d), F32),
        grid=(n // tm,),
        in_specs=[
            pl.BlockSpec((tm, d), lambda i: (i, 0)),
            pl.BlockSpec((1, d), lambda i: (0, 0)),
            pl.BlockSpec((d, 2 * d_ff), lambda i: (0, 0), pipeline_mode=once),
            pl.BlockSpec((d_ff, d), lambda i: (0, 0), pipeline_mode=once),
            pl.BlockSpec((1, d), lambda i: (0, 0)),
        ],
        out_specs=pl.BlockSpec((tm, d), lambda i: (i, 0)),
        compiler_params=_params(("parallel",)),
        name="ffn",
    )(x2d, g, w_gu, w_down, g_final)


def _rope_kernel(pos_ref, invf_ref, sgn_ref, cos_ref, sin_ref):
    ang = pos_ref[...].astype(F32) * invf_ref[...]
    cos_ref[...] = jnp.cos(ang)
    sin_ref[...] = jnp.sin(ang) * sgn_ref[...]


def _rope_tables(pos2d, *, tm=512):
    n = pos2d.shape[0]
    half = RET_DK // 2
    inv_freq = 1.0 / (ROPE_BASE ** jnp.linspace(0.0, 1.0, half, dtype=F32))
    invf = jnp.tile(jnp.concatenate([inv_freq, inv_freq]), RET_HEADS)[None, :]
    sgn = jnp.tile(jnp.concatenate([-jnp.ones((half,), F32), jnp.ones((half,), F32)]), RET_HEADS)[None, :]
    return pl.pallas_call(
        _rope_kernel,
        out_shape=(jax.ShapeDtypeStruct((n, QK), F32), jax.ShapeDtypeStruct((n, QK), F32)),
        grid=(n // tm,),
        in_specs=[
            pl.BlockSpec((tm, 1), lambda i: (i, 0)),
            pl.BlockSpec((1, QK), lambda i: (0, 0)),
            pl.BlockSpec((1, QK), lambda i: (0, 0)),
        ],
        out_specs=(pl.BlockSpec((tm, QK), lambda i: (i, 0)), pl.BlockSpec((tm, QK), lambda i: (i, 0))),
        compiler_params=_params(("parallel",)),
        name="rope_tables",
    )(pos2d, invf, sgn)


def _ret_log_gamma(width, head_w):
    head = _iota((1, width), 1) // head_w
    lg = jnp.zeros((1, width), F32)
    for h in range(RET_HEADS):
        lg = jnp.where(head == h, float(np.log(1.0 - 2.0 ** (-5.0 - h))), lg)
    return lg


def _inproj_kernel(x_ref, xp_ref, g_ref, w_ref, cos_ref, sin_ref, wa2_ref, ba_ref, cw_ref, mb_ref,
                   gq_ref, gk_ref, gla_ref, gv_ref, gr_ref, rq_ref, rks_ref, rkd_ref, rv_ref, rg_ref,
                   mq_ref, mk_ref, mv_ref, mo_ref, mi_ref, mf_ref, *, tiles_per_seq, C):
    tm = x_ref.shape[0]
    hn = _rms(x_ref[...], g_ref[...]).astype(BF16)

    def seg(off, width):
        return _dot(hn, w_ref[:, off:off + width])

    gq_ref[...] = seg(OFF_GQ, QK)
    gk_ref[...] = seg(OFF_GK, QK)
    gv_ref[...] = seg(OFF_GV, VW).astype(BF16)
    gr_ref[...] = _silu(seg(OFF_GR, VW)).astype(BF16)
    sm = seg(OFF_SM, SMALL_W)
    z = _dot(sm.astype(BF16), wa2_ref[...]) + ba_ref[...]
    gla_ref[...] = _log_sigmoid(z) * (1.0 / GLA_TAU)

    c = cos_ref[...]
    s = sin_ref[...]
    rq_ref[...] = (seg(OFF_RQ, QK) * c + seg(OFF_RQS, QK) * s).astype(BF16)
    rk = (seg(OFF_RK, QK) * c + seg(OFF_RKS, QK) * s) * (RET_DK ** -0.5)
    pos_in_chunk = (_iota((tm, 1), 0) & (C - 1)).astype(F32)
    k_dec = jnp.exp(_ret_log_gamma(QK, RET_DK) * (C - 1.0 - pos_in_chunk))
    rks_ref[...] = rk.astype(BF16)
    rkd_ref[...] = (rk * k_dec).astype(BF16)
    rv_ref[...] = seg(OFF_RV, VW).astype(BF16)
    rg_ref[...] = _silu(seg(OFF_RG, VW)).astype(BF16)

    x = seg(OFF_MQK, 2 * MLW)
    hp = _rms(xp_ref[...], g_ref[...]).astype(BF16)
    prev = _dot(hp, w_ref[:, OFF_MQK:OFF_MQK + 2 * MLW])
    prev = jnp.where(pl.program_id(0) % tiles_per_seq == 0, 0.0, prev)
    w = cw_ref[...]
    x8 = x[0:8, :]
    row8 = _iota((8, 2 * MLW), 0)
    acc = x * w[ML_CONV - 1:ML_CONV, :]
    acc8 = x8 * w[ML_CONV - 1:ML_CONV, :]
    for sft in range(1, ML_CONV):
        tap = w[ML_CONV - 1 - sft:ML_CONV - sft, :]
        acc = acc + pltpu.roll(x, sft, 0) * tap
        acc8 = acc8 + jnp.where(row8 < sft, pltpu.roll(prev, sft, 0), pltpu.roll(x8, sft, 0)) * tap
    y = _silu(jnp.concatenate([acc8, acc[8:, :]], axis=0))
    mq_ref[...] = y[:, :MLW].astype(BF16)
    mk_ref[...] = (y[:, MLW:] * (ML_DK ** -0.5)).astype(BF16)
    mv_ref[...] = seg(OFF_MV, MLW).astype(BF16)
    mo_ref[...] = _sigmoid(seg(OFF_MO, MLW)).astype(BF16)
    pre = sm + mb_ref[...]
    lane = _iota((1, SMALL_W), 1)
    gate = (lane >= SM_F) & (lane < SM_F + ML_HEADS)
    mi_ref[...] = jnp.where(gate, pltpu.roll(pre, SM_F - SM_I, 1), 0.0)
    mf_ref[...] = jnp.where(gate, _log_sigmoid(pre), 0.0)


def _inproj(x2d, g, w_cat, cos_t, sin_t, wa2p, ba, cw, mb, *, S, tm=512, C=MIX_CHUNK):
    n, d = x2d.shape
    widths = [(QK, F32), (QK, F32), (QK, F32), (VW, BF16), (VW, BF16),
              (QK, BF16), (QK, BF16), (QK, BF16), (VW, BF16), (VW, BF16),
              (MLW, BF16), (MLW, BF16), (MLW, BF16), (MLW, BF16), (SMALL_W, F32), (SMALL_W, F32)]
    row = lambda w: pl.BlockSpec((tm, w), lambda i: (i, 0))
    full = lambda a, c: pl.BlockSpec((a, c), lambda i: (0, 0))
    prev8 = pl.BlockSpec((8, d), lambda i: (jnp.maximum(i * (tm // 8) - 1, 0), 0))
    return pl.pallas_call(
        functools.partial(_inproj_kernel, tiles_per_seq=S // tm, C=C),
        out_shape=tuple(jax.ShapeDtypeStruct((n, w), dt) for w, dt in widths),
        grid=(n // tm,),
        in_specs=[row(d), prev8, full(1, d), full(d, W_CAT), row(QK), row(QK),
                  full(SMALL_W, QK), full(1, QK), full(ML_CONV, 2 * MLW), full(1, SMALL_W)],
        out_specs=tuple(row(w) for w, _ in widths),
        compiler_params=_params(("parallel",)),
        name="inproj",
    )(x2d, x2d, g, w_cat, cos_t, sin_t, wa2p, ba, cw, mb)


def _build_w_cat(w_in):
    L, d, _ = w_in.shape
    offs = np.concatenate([[0], np.cumsum(IN_SIZES)])
    col = lambda i: w_in[:, :, offs[i]:offs[i + 1]]
    (gq, gk, gv, ga, gr, rq, rk, rv, rg, mq, mk, mv, mo, mi, mf) = [col(i) for i in range(15)]

    def swap(t):
        half = RET_DK // 2
        return t.reshape(L, d, RET_HEADS, 2, half)[:, :, :, ::-1, :].reshape(L, d, QK)

    def pad(t, width):
        return jnp.pad(t, ((0, 0), (0, 0), (0, width - t.shape[-1])))

    small = jnp.concatenate([ga, mi, mf], axis=-1)
    pieces = [pad(gq, 256), pad(gk, 256), gv, gr, pad(rq, 256), pad(swap(rq), 256), pad(rk, 256),
              pad(swap(rk), 256), rv, rg, mq, mk, mv, mo, pad(small, SMALL_W)]
    w_cat = jnp.concatenate(pieces, axis=-1).astype(BF16)
    assert w_cat.shape[-1] == W_CAT
    return w_cat


def _pair_sum(x, lo):
    s_lo = jnp.sum(jnp.where(lo, x, 0.0), axis=-1, keepdims=True)
    s_hi = jnp.sum(jnp.where(lo, 0.0, x), axis=-1, keepdims=True)
    return jnp.where(lo, s_lo, s_hi)


def _qk_tiles(q_bf, k_bf, head_w):
    width = q_bf.shape[-1]
    out = []
    for h in range(width // head_w):
        t0 = (h * head_w) // LANES * LANES
        t1 = min(t0 + LANES, width)
        m = _lane_mask(t1 - t0, h * head_w - t0, (h + 1) * head_w - t0)
        out.append((q_bf[:, t0:t1] * m, k_bf[:, t0:t1]))
    return out


def _gla_kernel(q_ref, k_ref, la_ref, v_ref, r_ref, gn_ref, o_ref, st_ref, oacc_ref, *, T, C):
    @pl.when(pl.program_id(1) == 0)
    def _():
        st_ref[...] = jnp.zeros_like(st_ref)

    n_chunks = T // C
    causal = _iota((C, C), 0) >= _iota((C, C), 1)
    tri = causal.astype(BF16)
    bd = (_iota((VW, QK), 0) // GLA_DV) == (_iota((VW, QK), 1) // GLA_DK)
    lo_bf = _lane_mask(LANES, 0, GLA_DV)
    hi_bf = _lane_mask(LANES, GLA_DV, LANES)
    lo = _iota((1, LANES), 1) < GLA_DV
    scale = GLA_DK ** -0.5

    bcs = [_cumsum_rows(tri, la_ref[c * C:(c + 1) * C, :]) for c in range(n_chunks)]
    bl_min = bcs[0][C - 1:C, :]
    for c in range(1, n_chunks):
        bl_min = jnp.minimum(bl_min, bcs[c][C - 1:C, :])
    safe = jnp.min(bl_min) > -GLA_SAFE_LOG

    @pl.when(safe)
    def _():
        st = st_ref[...]
        for c in range(n_chunks):
            rows = slice(c * C, (c + 1) * C)
            bc = bcs[c]
            bl = bc[C - 1:C, :]
            k = k_ref[rows, :]
            qt = (q_ref[rows, :] * (scale * jnp.exp(bc))).astype(BF16)
            kt = (k * jnp.exp(-bc)).astype(BF16)
            kh = (k * jnp.exp(bl - bc)).astype(BF16)
            sc = [(_dot_nt(qm, kk).astype(BF16) * tri) for qm, kk in _qk_tiles(qt, kt, GLA_DK)]
            inter = _dot_nt(qt, st.astype(BF16))
            for p in range(GLA_HEADS // 2):
                cols = slice(LANES * p, LANES * (p + 1))
                vp = v_ref[rows, cols]
                oacc_ref[rows, cols] = (_dot(sc[2 * p], vp * lo_bf) + _dot(sc[2 * p + 1], vp * hi_bf)
                                        + inter[:, cols])
            st = st * jnp.exp(bl) + jnp.where(bd, _dot_tn(v_ref[rows, :], kh), 0.0)
        st_ref[...] = st

    @pl.when(jnp.logical_not(safe))
    def _():
        vb = v_ref[...]
        row_id = _iota((T, 2 * LANES), 0)

        def token(t, carry):
            a_t = jnp.exp(la_ref[pl.ds(t, 1), :])
            k_t = k_ref[pl.ds(t, 1), :]
            q_t = q_ref[pl.ds(t, 1), :] * scale
            v_col = _dot_tn(vb, (row_id == t).astype(BF16))[:, :QK]
            st = st_ref[...] * a_t + jnp.where(bd, v_col * k_t, 0.0)
            st_ref[...] = st
            q8 = jnp.broadcast_to(q_t, (8, QK)).astype(BF16)
            oacc_ref[pl.ds(t, 1), :] = _dot_nt(q8, st.astype(BF16))[0:1, :]
            return carry

        lax.fori_loop(0, T, token, 0)

    for p in range(GLA_HEADS // 2):
        cols = slice(LANES * p, LANES * (p + 1))
        o = oacc_ref[:, cols]
        ms = _pair_sum(o * o, lo) * (1.0 / GLA_DV)
        y = o * lax.rsqrt(ms + EPS) * gn_ref[:, cols] * r_ref[:, cols].astype(F32)
        o_ref[:, cols] = y.astype(BF16)


def _mix_specs(S, T):
    nt = S // T
    row = lambda w: pl.BlockSpec((T, w), lambda b, i: (b * nt + i, 0))
    full = lambda a, c: pl.BlockSpec((a, c), lambda b, i: (0, 0))
    return nt, row, full


def _gla(q, k, la, v, r, gn, *, B, S, T=MIX_TILE, C=MIX_CHUNK):
    nt, row, full = _mix_specs(S, T)
    return pl.pallas_call(
        functools.partial(_gla_kernel, T=T, C=C),
        out_shape=jax.ShapeDtypeStruct((B * S, VW), BF16),
        grid=(B, nt),
        in_specs=[row(QK), row(QK), row(QK), row(VW), row(VW), full(1, VW)],
        out_specs=row(VW),
        scratch_shapes=[pltpu.VMEM((VW, QK), F32), pltpu.VMEM((T, VW), F32)],
        compiler_params=_params(("arbitrary", "arbitrary")),
        name="gla",
    )(q, k, la, v, r, gn)


def _ret_kernel(q_ref, ks_ref, kd_ref, v_ref, rg_ref, gn_ref, o_ref, rt_ref, dec_ref, *, T, C):
    @pl.when(pl.program_id(1) == 0)
    def _():
        rt_ref[...] = jnp.zeros_like(rt_ref)
        rel = (_iota((C, C), 0) - _iota((C, C), 1)).astype(F32)
        for h in range(RET_HEADS):
            log_g = float(np.log(1.0 - 2.0 ** (-5.0 - h)))
            dec_ref[h] = jnp.where(rel >= 0.0, jnp.exp(log_g * jnp.maximum(rel, 0.0)), 0.0)

    lo_bf = _lane_mask(LANES, 0, RET_DV)
    hi_bf = _lane_mask(LANES, RET_DV, LANES)
    lo = _iota((1, LANES), 1) < RET_DV
    idx = _iota((C, 1), 0).astype(F32)
    q_dec = jnp.exp(_ret_log_gamma(VW, RET_DV) * (idx + 1.0))
    chunk_dec = jnp.exp(_ret_log_gamma(QK, RET_DK) * float(C))
    bd = (_iota((VW, QK), 0) // RET_DV) == (_iota((VW, QK), 1) // RET_DK)

    rt = rt_ref[...]
    for c in range(T // C):
        rows = slice(c * C, (c + 1) * C)
        q = q_ref[rows, :]
        sc = [(_dot_nt(qm, kk) * dec_ref[h]).astype(BF16)
              for h, (qm, kk) in enumerate(_qk_tiles(q, ks_ref[rows, :], RET_DK))]
        inter = _dot_nt(q, rt.astype(BF16)) * q_dec
        for p in range(RET_HEADS // 2):
            cols = slice(LANES * p, LANES * (p + 1))
            vp = v_ref[rows, cols]
            o = _dot(sc[2 * p], vp * lo_bf) + _dot(sc[2 * p + 1], vp * hi_bf) + inter[:, cols]
            mu = _pair_sum(o, lo) * (1.0 / RET_DV)
            xc = o - mu
            var = _pair_sum(xc * xc, lo) * (1.0 / RET_DV)
            y = xc * lax.rsqrt(var + EPS) * gn_ref[:, cols] * rg_ref[rows, cols].astype(F32)
            o_ref[rows, cols] = y.astype(BF16)
        rt = rt * chunk_dec + jnp.where(bd, _dot_tn(v_ref[rows, :], kd_ref[rows, :]), 0.0)
    rt_ref[...] = rt


def _ret(q, ks, kd, v, rg, gn, *, B, S, T=MIX_TILE, C=MIX_CHUNK):
    nt, row, full = _mix_specs(S, T)
    return pl.pallas_call(
        functools.partial(_ret_kernel, T=T, C=C),
        out_shape=jax.ShapeDtypeStruct((B * S, VW), BF16),
        grid=(B, nt),
        in_specs=[row(QK), row(QK), row(QK), row(VW), row(VW), full(1, VW)],
        out_specs=row(VW),
        scratch_shapes=[pltpu.VMEM((VW, QK), F32), pltpu.VMEM((RET_HEADS, C, C), F32)],
        compiler_params=_params(("arbitrary", "arbitrary")),
        name="retention",
    )(q, ks, kd, v, rg, gn)


def _mlstm_kernel(q_ref, k_ref, v_ref, og_ref, mi_ref, mf_ref, o_ref, ct_ref, n_ref, m_ref, *, T, C):
    @pl.when(pl.program_id(1) == 0)
    def _():
        ct_ref[...] = jnp.zeros_like(ct_ref)
        n_ref[...] = jnp.zeros_like(n_ref)
        m_ref[...] = jnp.zeros_like(m_ref)

    causal = _iota((C, C), 0) >= _iota((C, C), 1)
    tri = causal.astype(BF16)
    bd = (_iota((MLW, MLW), 0) // ML_DV) == (_iota((MLW, MLW), 1) // ML_DK)
    bd_bf = bd.astype(BF16)
    lo_bf = _lane_mask(LANES, 0, ML_DV)
    hi_bf = _lane_mask(LANES, ML_DV, LANES)
    lo = _iota((1, LANES), 1) < ML_DV
    widen = (_iota((LANES, MLW), 0) == (_iota((LANES, MLW), 1) // ML_DV + SM_F)).astype(BF16)
    ones_blk = jnp.ones((C, LANES), BF16)
    row_id = _iota((C, LANES), 0)

    ct = ct_ref[...]
    n_row = n_ref[...]
    m_prev = m_ref[...]
    for c in range(T // C):
        rows = slice(c * C, (c + 1) * C)
        i_pre = mi_ref[rows, :]
        f_cum = _cumsum_rows(tri, mf_ref[rows, :])
        g = i_pre - f_cum
        g_t = g.T
        g_max = g
        sft = 1
        while sft < C:
            g_max = jnp.maximum(g_max, jnp.where(row_id >= sft, pltpu.roll(g_max, sft, 0), -jnp.inf))
            sft *= 2
        mm = jnp.maximum(g_max, m_prev)
        b_last = f_cum[C - 1:C, :]
        log_w = b_last - f_cum + i_pre
        m_new = jnp.maximum(b_last + m_prev, jnp.max(log_w, axis=0, keepdims=True))
        g_prev = jnp.exp(b_last + m_prev - m_new)
        slab = jnp.concatenate([jnp.exp(m_prev - mm), jnp.exp(-(f_cum + mm)), jnp.exp(log_w - m_new),
                                jnp.broadcast_to(g_prev, (8, LANES))], axis=0)
        s_hi, s_lo = _split2(slab)
        wide = _dot(s_hi, widen) + _dot(s_lo, widen)
        w_inter, e_negm, w_state, gf = wide[0:C], wide[C:2 * C], wide[2 * C:3 * C], wide[3 * C:3 * C + 1]

        q = q_ref[rows, :]
        k = k_ref[rows, :]
        res = []
        for h, (qm, kk) in enumerate(_qk_tiles(q, k, ML_DK)):
            lane = SM_F + h
            dm = jnp.where(causal, jnp.exp(g_t[lane:lane + 1, :] - mm[:, lane:lane + 1]), 0.0)
            s = (_dot_nt(qm, kk) * dm).astype(BF16)
            vp = v_ref[rows, LANES * (h // 2):LANES * (h // 2 + 1)]
            v_ext = jnp.concatenate([vp * (lo_bf if h % 2 == 0 else hi_bf), ones_blk], axis=1)
            res.append(_dot(s, v_ext))
        n_mat = jnp.broadcast_to(n_row.astype(BF16), (MLW, MLW)) * bd_bf
        inter = _dot_nt(q, jnp.concatenate([ct.astype(BF16), n_mat], axis=0))
        for p in range(ML_HEADS // 2):
            cols = slice(LANES * p, LANES * (p + 1))
            ra, rb = res[2 * p], res[2 * p + 1]
            num = ra[:, :LANES] + rb[:, :LANES] + w_inter[:, cols] * inter[:, cols]
            den = (jnp.where(lo, ra[:, LANES:], rb[:, LANES:])
                   + w_inter[:, cols] * inter[:, MLW + LANES * p:MLW + LANES * (p + 1)])
            h_t = num / jnp.maximum(jnp.abs(den), e_negm[:, cols])
            o_ref[rows, cols] = (og_ref[rows, cols].astype(F32) * h_t).astype(BF16)

        wk = w_state * k.astype(F32)
        ct = ct * gf + jnp.where(bd, _dot_tn(v_ref[rows, :], wk.astype(BF16)), 0.0)
        n_row = n_row * gf + jnp.sum(wk, axis=0, keepdims=True)
        m_prev = m_new
    ct_ref[...] = ct
    n_ref[...] = n_row
    m_ref[...] = m_prev


def _mlstm(q, k, v, og, mi, mf, *, B, S, T=MIX_TILE, C=MIX_CHUNK):
    nt, row, full = _mix_specs(S, T)
    return pl.pallas_call(
        functools.partial(_mlstm_kernel, T=T, C=C),
        out_shape=jax.ShapeDtypeStruct((B * S, MLW), BF16),
        grid=(B, nt),
        in_specs=[row(MLW), row(MLW), row(MLW), row(MLW), row(SMALL_W), row(SMALL_W)],
        out_specs=row(MLW),
        scratch_shapes=[pltpu.VMEM((MLW, MLW), F32), pltpu.VMEM((1, MLW), F32), pltpu.VMEM((1, LANES), F32)],
        compiler_params=_params(("arbitrary", "arbitrary")),
        name="mlstm",
    )(q, k, v, og, mi, mf)


def _memkv_kernel(mem_ref, g_ref, w_ref, k_ref, v_ref):
    mn = _rms(mem_ref[0], g_ref[...]).astype(BF16)
    d = mem_ref.shape[-1]
    k_ref[0] = _dot(mn, w_ref[:, :d]).astype(BF16)
    v_ref[0] = _dot(mn, w_ref[:, d:]).astype(BF16)


def _memkv(mem, g, w_kv):
    b, m, d = mem.shape
    blk = pl.BlockSpec((1, m, d), lambda i: (i, 0, 0))
    return pl.pallas_call(
        _memkv_kernel,
        out_shape=(jax.ShapeDtypeStruct((b, m, d), BF16), jax.ShapeDtypeStruct((b, m, d), BF16)),
        grid=(b,),
        in_specs=[blk, pl.BlockSpec((1, d), lambda i: (0, 0)), pl.BlockSpec((d, 2 * d), lambda i: (0, 0))],
        out_specs=(blk, blk),
        compiler_params=_params(("parallel",)),
        name="mem_kv",
    )(mem, g, w_kv)


def _outxa_kernel(x_ref, og_ref, or_ref, om_ref, wo_ref, g_ref, wq_ref, k_ref, v_ref, wxo_ref, o_ref):
    x1 = (x_ref[...] + _dot(og_ref[...], wo_ref[0:VW, :]) + _dot(or_ref[...], wo_ref[VW:2 * VW, :])
          + _dot(om_ref[...], wo_ref[2 * VW:, :]))
    hn = _rms(x1, g_ref[...]).astype(BF16)
    q = (_dot(hn, wq_ref[...]) * (XA_DH ** -0.5)).astype(BF16)
    outs = []
    for h in range(XA_HEADS):
        cols = slice(XA_DH * h, XA_DH * (h + 1))
        s = _dot_nt(q[:, cols], k_ref[0, :, cols])
        s = s - jnp.max(s, axis=-1, keepdims=True)
        e = jnp.exp(s)
        p = e / jnp.sum(e, axis=-1, keepdims=True)
        outs.append(_dot(p.astype(BF16), v_ref[0, :, cols]).astype(BF16))
    o = jnp.concatenate(outs, axis=-1)
    o_ref[...] = x1 + _dot(o, wxo_ref[...])


def _outxa(x2d, o_gla, o_ret, o_ml, w_out, g_xa, w_q, mem_k, mem_v, w_o, *, B, S, tm=512):
    n, d = x2d.shape
    nt = S // tm
    m = mem_k.shape[1]
    row = lambda w: pl.BlockSpec((tm, w), lambda b, i: (b * nt + i, 0))
    full = lambda a, c: pl.BlockSpec((a, c), lambda b, i: (0, 0))
    kv = pl.BlockSpec((1, m, d), lambda b, i: (b, 0, 0))
    return pl.pallas_call(
        _outxa_kernel,
        out_shape=jax.ShapeDtypeStruct((n, d), F32),
        grid=(B, nt),
        in_specs=[row(d), row(VW), row(VW), row(MLW), full(d, d), full(1, d), full(d, d), kv, kv, full(d, d)],
        out_specs=row(d),
        compiler_params=_params(("parallel", "parallel")),
        name="outproj_xattn",
    )(x2d, o_gla, o_ret, o_ml, w_out, g_xa, w_q, mem_k, mem_v, w_o)


def kernel(x, mem, positions, g_ffa, w_ffa_gu, w_ffa_down, g_mix, w_in, gla_w_a2, gla_b_a, gla_g_norm,
           ret_g_norm, ml_conv, ml_b_i, ml_b_f, w_out, g_xa, g_mem, w_xa_q, w_xa_kv, w_xa_o, g_ffb,
           w_ffb_gu, w_ffb_down, g_final):
    B, S, D = x.shape
    L = w_in.shape[0]
    N = B * S
    bf = lambda t: t.astype(BF16)
    w_ffa_gu, w_ffa_down, w_ffb_gu, w_ffb_down = bf(w_ffa_gu), bf(w_ffa_down), bf(w_ffb_gu), bf(w_ffb_down)
    w_out_b, w_q_b, w_kv_b, w_o_b = bf(w_out), bf(w_xa_q), bf(w_xa_kv), bf(w_xa_o)
    w_cat = _build_w_cat(w_in)
    wa2p = bf(jnp.pad(gla_w_a2, ((0, 0), (0, SMALL_W - GLA_RANK), (0, 0))))
    ml_bias = jnp.pad(jnp.concatenate([ml_b_i, ml_b_f], axis=-1),
                      ((0, 0), (SM_I, SMALL_W - SM_I - 2 * ML_HEADS)))
    row = lambda t: t[None, :]

    cos_t, sin_t = _rope_tables(positions.reshape(N, 1))
    h = x.reshape(N, D)
    for l in range(L):
        h = _ffn(h, row(g_ffa[l]), w_ffa_gu[l], w_ffa_down[l], row(g_final), final=False)
        (gq, gk, gla, gv, gr, rq, rks, rkd, rv, rg, mq, mk, mv, mo, mi, mf) = _inproj(
            h, row(g_mix[l]), w_cat[l], cos_t, sin_t, wa2p[l], row(gla_b_a[l]), ml_conv[l], row(ml_bias[l]), S=S)
        o_gla = _gla(gq, gk, gla, gv, gr, row(gla_g_norm[l]), B=B, S=S)
        o_ret = _ret(rq, rks, rkd, rv, rg, row(ret_g_norm[l]), B=B, S=S)
        o_ml = _mlstm(mq, mk, mv, mo, mi, mf, B=B, S=S)
        mem_k, mem_v = _memkv(mem, row(g_mem[l]), w_kv_b[l])
        h = _outxa(h, o_gla, o_ret, o_ml, w_out_b[l], row(g_xa[l]), w_q_b[l], mem_k, mem_v, w_o_b[l], B=B, S=S)
        h = _ffn(h, row(g_ffb[l]), w_ffb_gu[l], w_ffb_down[l], row(g_final), final=(l == L - 1))
    return h.reshape(B, S, D)
```

```python
import functools

import numpy as np
import jax
import jax.numpy as jnp
from jax import lax
from jax.experimental import pallas as pl
from jax.experimental.pallas import tpu as pltpu

F32 = jnp.float32
BF16 = jnp.bfloat16
EPS = 1e-6

D_MODEL = 1024
GLA_HEADS, GLA_DK, GLA_DV, GLA_RANK, GLA_TAU = 6, 32, 64, 16, 16.0
RET_HEADS, RET_DK, RET_DV = 6, 32, 64
ML_HEADS, ML_DK, ML_DV, ML_CONV = 4, 64, 64, 4
ROPE_BASE = 10000.0
XA_HEADS = 4
XA_DH = D_MODEL // XA_HEADS
IN_SIZES = (
    GLA_HEADS * GLA_DK, GLA_HEADS * GLA_DK, GLA_HEADS * GLA_DV, GLA_RANK, GLA_HEADS * GLA_DV,
    RET_HEADS * RET_DK, RET_HEADS * RET_DK, RET_HEADS * RET_DV, RET_HEADS * RET_DV,
    ML_HEADS * ML_DK, ML_HEADS * ML_DK, ML_HEADS * ML_DV, ML_HEADS * ML_DV, ML_HEADS, ML_HEADS,
)

LANES = 128
QK = GLA_HEADS * GLA_DK
VW = GLA_HEADS * GLA_DV
MLW = ML_HEADS * ML_DK
SMALL_W = LANES
SM_I = GLA_RANK
SM_F = GLA_RANK + ML_HEADS

OFF_GQ, OFF_GK, OFF_GV, OFF_GR = 0, 256, 512, 896
OFF_RQ, OFF_RQS, OFF_RK, OFF_RKS, OFF_RV, OFF_RG = 1280, 1536, 1792, 2048, 2304, 2688
OFF_MQK, OFF_MV, OFF_MO, OFF_SM = 3072, 3584, 3840, 4096
W_CAT = 4224

MIX_CHUNK = 128
MIX_TILE = 512
GLA_SAFE_LOG = 40.0
VMEM_LIMIT = 48 * 1024 * 1024


def _dot(a, b):
    return jnp.dot(a, b, preferred_element_type=F32)


def _dot_nt(a, b):
    return lax.dot_general(a, b, (((1,), (1,)), ((), ())), preferred_element_type=F32)


def _dot_tn(a, b):
    return lax.dot_general(a, b, (((0,), (0,)), ((), ())), preferred_element_type=F32)


def _sigmoid(x):
    return 1.0 / (1.0 + jnp.exp(-x))


def _silu(x):
    return x * _sigmoid(x)


def _log_sigmoid(x):
    return jnp.minimum(x, 0.0) - jnp.log1p(jnp.exp(-jnp.abs(x)))


def _rms(x, g):
    return x * lax.rsqrt(jnp.mean(x * x, axis=-1, keepdims=True) + EPS) * g


def _split2(x):
    hi = x.astype(BF16)
    return hi, (x - hi.astype(F32)).astype(BF16)


def _split3(x):
    hi = x.astype(BF16)
    r1 = x - hi.astype(F32)
    mid = r1.astype(BF16)
    return hi, mid, (r1 - mid.astype(F32)).astype(BF16)


def _cumsum_rows(tri_bf, x):
    hi, mid, lo = _split3(x)
    return _dot(tri_bf, hi) + _dot(tri_bf, mid) + _dot(tri_bf, lo)


def _iota(shape, dim):
    return lax.broadcasted_iota(jnp.int32, shape, dim)


def _lane_mask(width, lo, hi, dtype=BF16):
    lane = _iota((1, width), 1)
    return ((lane >= lo) & (lane < hi)).astype(dtype)


def _params(sem):
    return pltpu.CompilerParams(dimension_semantics=sem, vmem_limit_bytes=VMEM_LIMIT)


def _layer(l, tail, grid_rank):
    idx = (l,) + (0,) * len(tail)
    imap = (lambda i: idx) if grid_rank == 1 else (lambda b, i: idx)
    return pl.BlockSpec((None,) + tuple(tail), imap, pipeline_mode=pl.Buffered(1))


def _const(shape, grid_rank):
    idx = (0,) * len(shape)
    imap = (lambda i: idx) if grid_rank == 1 else (lambda b, i: idx)
    return pl.BlockSpec(tuple(shape), imap, pipeline_mode=pl.Buffered(1))


FFN_SUB = 256
FFN_CHUNK = 1024


def _ffn_kernel(x_ref, g_ref, wgu_ref, wd_ref, gf_ref, o_ref, *, final):
    tm = x_ref.shape[0]
    d_ff = wd_ref.shape[0]
    for r in range(tm // FFN_SUB):
        rows = slice(r * FFN_SUB, (r + 1) * FFN_SUB)
        x = x_ref[rows, :]
        hn = _rms(x, g_ref[...]).astype(BF16)
        acc = None
        for off in range(0, d_ff, FFN_CHUNK):
            cw = min(FFN_CHUNK, d_ff - off)
            a = _dot(hn, wgu_ref[:, off:off + cw])
            g = _dot(hn, wgu_ref[:, d_ff + off:d_ff + off + cw])
            part = _dot((_silu(a) * g).astype(BF16), wd_ref[off:off + cw, :])
            acc = part if acc is None else acc + part
        y = x + 0.5 * acc
        if final:
            y = _rms(y, gf_ref[...])
        o_ref[rows, :] = y


def _ffn(x2d, l, g, w_gu, w_down, g_final, *, final, tm=512):
    n, d = x2d.shape
    d_ff = w_down.shape[1]
    return pl.pallas_call(
        functools.partial(_ffn_kernel, final=final),
        out_shape=jax.ShapeDtypeStruct((n, d), F32),
        grid=(n // tm,),
        in_specs=[
            pl.BlockSpec((tm, d), lambda i: (i, 0)),
            _layer(l, (1, d), 1),
            _layer(l, (d, 2 * d_ff), 1),
            _layer(l, (d_ff, d), 1),
            _const((1, d), 1),
        ],
        out_specs=pl.BlockSpec((tm, d), lambda i: (i, 0)),
        compiler_params=_params(("parallel",)),
        name="ffn",
    )(x2d, g, w_gu, w_down, g_final)


def _rope_kernel(pos_ref, invf_ref, spread_ref, sgn_ref, cos_ref, sin_ref):
    ang = invf_ref[...] * pos_ref[...].astype(F32)
    spread = spread_ref[...]

    def to_rows(t):
        return sum(_dot_tn(part, spread) for part in _split3(t))

    cos_ref[...] = to_rows(jnp.cos(ang))
    sin_ref[...] = to_rows(jnp.sin(ang)) * sgn_ref[...]


def _rope_tables(pos_row, *, tm=512):
    n = pos_row.shape[1]
    half = RET_DK // 2
    inv_freq = 1.0 / (ROPE_BASE ** jnp.linspace(0.0, 1.0, half, dtype=F32))
    spread = (np.arange(half)[:, None] == (np.arange(QK)[None, :] % half)).astype(np.float32)
    sgn = np.tile(np.concatenate([-np.ones(half, np.float32), np.ones(half, np.float32)]), RET_HEADS)[None, :]
    return pl.pallas_call(
        _rope_kernel,
        out_shape=(jax.ShapeDtypeStruct((n, QK), F32), jax.ShapeDtypeStruct((n, QK), F32)),
        grid=(n // tm,),
        in_specs=[
            pl.BlockSpec((1, tm), lambda i: (0, i)),
            _const((half, 1), 1),
            _const((half, QK), 1),
            _const((1, QK), 1),
        ],
        out_specs=(pl.BlockSpec((tm, QK), lambda i: (i, 0)), pl.BlockSpec((tm, QK), lambda i: (i, 0))),
        compiler_params=_params(("parallel",)),
        name="rope_tables",
    )(pos_row, inv_freq[:, None], jnp.asarray(spread, BF16), jnp.asarray(sgn))


INPROJ_SUB = 256


def _ret_log_gamma(width, head_w):
    head = _iota((1, width), 1) // head_w
    lg = jnp.zeros((1, width), F32)
    for h in range(RET_HEADS):
        lg = jnp.where(head == h, float(np.log(1.0 - 2.0 ** (-5.0 - h))), lg)
    return lg


def _inproj_kernel(x_ref, xp_ref, g_ref, w_ref, cos_ref, sin_ref, wa2_ref, ba_ref, cw_ref, mb_ref,
                   gq_ref, gk_ref, gla_ref, gv_ref, gr_ref, rq_ref, rks_ref, rkd_ref, rv_ref, rg_ref,
                   mq_ref, mk_ref, mv_ref, mo_ref, mi_ref, mf_ref, *, tiles_per_seq, C):
    tm = x_ref.shape[0]
    sub = INPROJ_SUB
    pos_in_chunk = (_iota((sub, 1), 0) & (C - 1)).astype(F32)
    k_dec = jnp.exp(_ret_log_gamma(QK, RET_DK) * (C - 1.0 - pos_in_chunk))
    lane = _iota((1, SMALL_W), 1)
    gate = (lane >= SM_F) & (lane < SM_F + ML_HEADS)
    conv_in = []
    for r in range(tm // sub):
        rows = slice(r * sub, (r + 1) * sub)
        hn = _rms(x_ref[rows, :], g_ref[...]).astype(BF16)

        def seg(off, width):
            return _dot(hn, w_ref[:, off:off + width])

        gq_ref[rows, :] = seg(OFF_GQ, QK)
        gk_ref[rows, :] = seg(OFF_GK, QK)
        gv_ref[rows, :] = seg(OFF_GV, VW).astype(BF16)
        gr_ref[rows, :] = _silu(seg(OFF_GR, VW)).astype(BF16)
        sm = seg(OFF_SM, SMALL_W)
        z = _dot(sm.astype(BF16), wa2_ref[...]) + ba_ref[...]
        gla_ref[rows, :] = _log_sigmoid(z) * (1.0 / GLA_TAU)

        c = cos_ref[rows, :]
        s = sin_ref[rows, :]
        rq_ref[rows, :] = (seg(OFF_RQ, QK) * c + seg(OFF_RQS, QK) * s).astype(BF16)
        rk = (seg(OFF_RK, QK) * c + seg(OFF_RKS, QK) * s) * (RET_DK ** -0.5)
        rks_ref[rows, :] = rk.astype(BF16)
        rkd_ref[rows, :] = (rk * k_dec).astype(BF16)
        rv_ref[rows, :] = seg(OFF_RV, VW).astype(BF16)
        rg_ref[rows, :] = _silu(seg(OFF_RG, VW)).astype(BF16)

        conv_in.append(seg(OFF_MQK, 2 * MLW))
        mv_ref[rows, :] = seg(OFF_MV, MLW).astype(BF16)
        mo_ref[rows, :] = _sigmoid(seg(OFF_MO, MLW)).astype(BF16)
        pre = sm + mb_ref[...]
        mi_ref[rows, :] = jnp.where(gate, pltpu.roll(pre, SM_F - SM_I, 1), 0.0)
        mf_ref[rows, :] = jnp.where(gate, _log_sigmoid(pre), 0.0)

    x = jnp.concatenate(conv_in, axis=0)
    hp = _rms(xp_ref[...], g_ref[...]).astype(BF16)
    prev = _dot(hp, w_ref[:, OFF_MQK:OFF_MQK + 2 * MLW])
    prev = jnp.where(pl.program_id(0) % tiles_per_seq == 0, 0.0, prev)
    w = cw_ref[...]
    x8 = x[0:8, :]
    row8 = _iota((8, 2 * MLW), 0)
    acc = x * w[ML_CONV - 1:ML_CONV, :]
    acc8 = x8 * w[ML_CONV - 1:ML_CONV, :]
    for sft in range(1, ML_CONV):
        tap = w[ML_CONV - 1 - sft:ML_CONV - sft, :]
        acc = acc + pltpu.roll(x, sft, 0) * tap
        acc8 = acc8 + jnp.where(row8 < sft, pltpu.roll(prev, sft, 0), pltpu.roll(x8, sft, 0)) * tap
    y = _silu(jnp.concatenate([acc8, acc[8:, :]], axis=0))
    mq_ref[...] = y[:, :MLW].astype(BF16)
    mk_ref[...] = (y[:, MLW:] * (ML_DK ** -0.5)).astype(BF16)


def _inproj(x2d, l, g, w_cat, cos_t, sin_t, wa2p, ba, cw, mb, *, S, tm=512, C=MIX_CHUNK):
    n, d = x2d.shape
    widths = [(QK, F32), (QK, F32), (QK, F32), (VW, BF16), (VW, BF16),
              (QK, BF16), (QK, BF16), (QK, BF16), (VW, BF16), (VW, BF16),
              (MLW, BF16), (MLW, BF16), (MLW, BF16), (MLW, BF16), (SMALL_W, F32), (SMALL_W, F32)]
    row = lambda w: pl.BlockSpec((tm, w), lambda i: (i, 0))
    prev8 = pl.BlockSpec((8, d), lambda i: (jnp.maximum(i * (tm // 8) - 1, 0), 0))
    return pl.pallas_call(
        functools.partial(_inproj_kernel, tiles_per_seq=S // tm, C=C),
        out_shape=tuple(jax.ShapeDtypeStruct((n, w), dt) for w, dt in widths),
        grid=(n // tm,),
        in_specs=[row(d), prev8, _layer(l, (1, d), 1), _layer(l, (d, W_CAT), 1), row(QK), row(QK),
                  _layer(l, (SMALL_W, QK), 1), _layer(l, (1, QK), 1), _layer(l, (ML_CONV, 2 * MLW), 1),
                  _layer(l, (1, SMALL_W), 1)],
        out_specs=tuple(row(w) for w, _ in widths),
        compiler_params=_params(("parallel",)),
        name="inproj",
    )(x2d, x2d, g, w_cat, cos_t, sin_t, wa2p, ba, cw, mb)


def _build_w_cat(w_in):
    L, d, _ = w_in.shape
    offs = np.concatenate([[0], np.cumsum(IN_SIZES)])
    col = lambda i: w_in[:, :, offs[i]:offs[i + 1]]
    (gq, gk, gv, ga, gr, rq, rk, rv, rg, mq, mk, mv, mo, mi, mf) = [col(i) for i in range(15)]

    def swap(t):
        half = RET_DK // 2
        return t.reshape(L, d, RET_HEADS, 2, half)[:, :, :, ::-1, :].reshape(L, d, QK)

    def pad(t, width):
        return jnp.pad(t, ((0, 0), (0, 0), (0, width - t.shape[-1])))

    small = jnp.concatenate([ga, mi, mf], axis=-1)
    pieces = [pad(gq, 256), pad(gk, 256), gv, gr, pad(rq, 256), pad(swap(rq), 256), pad(rk, 256),
              pad(swap(rk), 256), rv, rg, mq, mk, mv, mo, pad(small, SMALL_W)]
    w_cat = jnp.concatenate(pieces, axis=-1).astype(BF16)
    assert w_cat.shape[-1] == W_CAT
    return w_cat


def _pair_sum(x, lo):
    s_lo = jnp.sum(jnp.where(lo, x, 0.0), axis=-1, keepdims=True)
    s_hi = jnp.sum(jnp.where(lo, 0.0, x), axis=-1, keepdims=True)
    return jnp.where(lo, s_lo, s_hi)


def _qk_tiles(q_bf, k_bf, head_w):
    width = q_bf.shape[-1]
    out = []
    for h in range(width // head_w):
        t0 = (h * head_w) // LANES * LANES
        t1 = min(t0 + LANES, width)
        m = _lane_mask(t1 - t0, h * head_w - t0, (h + 1) * head_w - t0)
        out.append((q_bf[:, t0:t1] * m, k_bf[:, t0:t1]))
    return out


def _gla_kernel(q_ref, k_ref, la_ref, v_ref, r_ref, gn_ref, o_ref, st_ref, oacc_ref, *, T, C):
    @pl.when(pl.program_id(1) == 0)
    def _():
        st_ref[...] = jnp.zeros_like(st_ref)

    n_chunks = T // C
    causal = _iota((C, C), 0) >= _iota((C, C), 1)
    tri = causal.astype(BF16)
    bd = (_iota((VW, QK), 0) // GLA_DV) == (_iota((VW, QK), 1) // GLA_DK)
    lo_bf = _lane_mask(LANES, 0, GLA_DV)
    hi_bf = _lane_mask(LANES, GLA_DV, LANES)
    lo = _iota((1, LANES), 1) < GLA_DV
    scale = GLA_DK ** -0.5

    bl_min = jnp.sum(la_ref[0:C, :], axis=0, keepdims=True)
    for c in range(1, n_chunks):
        bl_min = jnp.minimum(bl_min, jnp.sum(la_ref[c * C:(c + 1) * C, :], axis=0, keepdims=True))
    safe = jnp.min(bl_min) > -GLA_SAFE_LOG

    def finish(o, rws, cols):
        ms = _pair_sum(o * o, lo) * (1.0 / GLA_DV)
        return (o * lax.rsqrt(ms + EPS) * gn_ref[:, cols] * r_ref[rws, cols].astype(F32)).astype(BF16)

    @pl.when(safe)
    def _():
        bcs = [_cumsum_rows(tri, la_ref[c * C:(c + 1) * C, :]) for c in range(n_chunks)]
        st = st_ref[...]
        for c in range(n_chunks):
            rows = slice(c * C, (c + 1) * C)
            bc = bcs[c]
            bl = bc[C - 1:C, :]
            k = k_ref[rows, :]
            qt = (q_ref[rows, :] * (scale * jnp.exp(bc))).astype(BF16)
            kt = (k * jnp.exp(-bc)).astype(BF16)
            kh = (k * jnp.exp(bl - bc)).astype(BF16)
            sc = [(_dot_nt(qm, kk).astype(BF16) * tri) for qm, kk in _qk_tiles(qt, kt, GLA_DK)]
            inter = _dot_nt(qt, st.astype(BF16))
            for p in range(GLA_HEADS // 2):
                cols = slice(LANES * p, LANES * (p + 1))
                vp = v_ref[rows, cols]
                o = _dot(sc[2 * p], vp * lo_bf) + _dot(sc[2 * p + 1], vp * hi_bf) + inter[:, cols]
                o_ref[rows, cols] = finish(o, rows, cols)
            st = st * jnp.exp(bl) + jnp.where(bd, _dot_tn(v_ref[rows, :], kh), 0.0)
        st_ref[...] = st

    @pl.when(jnp.logical_not(safe))
    def _():
        vb = v_ref[...]
        row_id = _iota((T, 2 * LANES), 0)

        def token(t, carry):
            a_t = jnp.exp(la_ref[pl.ds(t, 1), :])
            k_t = k_ref[pl.ds(t, 1), :]
            q_t = q_ref[pl.ds(t, 1), :] * scale
            v_col = _dot_tn(vb, (row_id == t).astype(BF16))[:, :QK]
            st = st_ref[...] * a_t + jnp.where(bd, v_col * k_t, 0.0)
            st_ref[...] = st
            q8 = jnp.broadcast_to(q_t, (8, QK)).astype(BF16)
            oacc_ref[pl.ds(t, 1), :] = _dot_nt(q8, st.astype(BF16))[0:1, :]
            return carry

        lax.fori_loop(0, T, token, 0)
        for p in range(GLA_HEADS // 2):
            cols = slice(LANES * p, LANES * (p + 1))
            o_ref[:, cols] = finish(oacc_ref[:, cols], slice(0, T), cols)


def _mix_specs(S, T):
    nt = S // T
    row = lambda w: pl.BlockSpec((T, w), lambda b, i: (b * nt + i, 0))
    full = lambda a, c: pl.BlockSpec((a, c), lambda b, i: (0, 0))
    return nt, row, full


def _gla(q, k, la, v, r, l, gn, *, B, S, T=MIX_TILE, C=MIX_CHUNK):
    nt, row, full = _mix_specs(S, T)
    return pl.pallas_call(
        functools.partial(_gla_kernel, T=T, C=C),
        out_shape=jax.ShapeDtypeStruct((B * S, VW), BF16),
        grid=(B, nt),
        in_specs=[row(QK), row(QK), row(QK), row(VW), row(VW), _layer(l, (1, VW), 2)],
        out_specs=row(VW),
        scratch_shapes=[pltpu.VMEM((VW, QK), F32), pltpu.VMEM((T, VW), F32)],
        compiler_params=_params(("arbitrary", "arbitrary")),
        name="gla",
    )(q, k, la, v, r, gn)


def _ret_kernel(q_ref, ks_ref, kd_ref, v_ref, rg_ref, gn_ref, o_ref, rt_ref, dec_ref, *, T, C):
    @pl.when(pl.program_id(1) == 0)
    def _():
        rt_ref[...] = jnp.zeros_like(rt_ref)
        rel = (_iota((C, C), 0) - _iota((C, C), 1)).astype(F32)
        for h in range(RET_HEADS):
            log_g = float(np.log(1.0 - 2.0 ** (-5.0 - h)))
            dec_ref[h] = jnp.where(rel >= 0.0, jnp.exp(log_g * jnp.maximum(rel, 0.0)), 0.0)

    lo_bf = _lane_mask(LANES, 0, RET_DV)
    hi_bf = _lane_mask(LANES, RET_DV, LANES)
    lo = _iota((1, LANES), 1) < RET_DV
    idx = _iota((C, 1), 0).astype(F32)
    q_dec = jnp.exp(_ret_log_gamma(VW, RET_DV) * (idx + 1.0))
    chunk_dec = jnp.exp(_ret_log_gamma(QK, RET_DK) * float(C))
    bd = (_iota((VW, QK), 0) // RET_DV) == (_iota((VW, QK), 1) // RET_DK)

    rt = rt_ref[...]
    for c in range(T // C):
        rows = slice(c * C, (c + 1) * C)
        q = q_ref[rows, :]
        sc = [(_dot_nt(qm, kk) * dec_ref[h]).astype(BF16)
              for h, (qm, kk) in enumerate(_qk_tiles(q, ks_ref[rows, :], RET_DK))]
        inter = _dot_nt(q, rt.astype(BF16)) * q_dec
        for p in range(RET_HEADS // 2):
            cols = slice(LANES * p, LANES * (p + 1))
            vp = v_ref[rows, cols]
            o = _dot(sc[2 * p], vp * lo_bf) + _dot(sc[2 * p + 1], vp * hi_bf) + inter[:, cols]
            mu = _pair_sum(o, lo) * (1.0 / RET_DV)
            xc = o - mu
            var = _pair_sum(xc * xc, lo) * (1.0 / RET_DV)
            y = xc * lax.rsqrt(var + EPS) * gn_ref[:, cols] * rg_ref[rows, cols].astype(F32)
            o_ref[rows, cols] = y.astype(BF16)
        rt = rt * chunk_dec + jnp.where(bd, _dot_tn(v_ref[rows, :], kd_ref[rows, :]), 0.0)
    rt_ref[...] = rt


def _ret(q, ks, kd, v, rg, l, gn, *, B, S, T=MIX_TILE, C=MIX_CHUNK):
    nt, row, full = _mix_specs(S, T)
    return pl.pallas_call(
        functools.partial(_ret_kernel, T=T, C=C),
        out_shape=jax.ShapeDtypeStruct((B * S, VW), BF16),
        grid=(B, nt),
        in_specs=[row(QK), row(QK), row(QK), row(VW), row(VW), _layer(l, (1, VW), 2)],
        out_specs=row(VW),
        scratch_shapes=[pltpu.VMEM((VW, QK), F32), pltpu.VMEM((RET_HEADS, C, C), F32)],
        compiler_params=_params(("arbitrary", "arbitrary")),
        name="retention",
    )(q, ks, kd, v, rg, gn)


def _mlstm_kernel(q_ref, k_ref, v_ref, og_ref, mi_ref, mf_ref, o_ref, ct_ref, n_ref, m_ref, *, T, C):
    @pl.when(pl.program_id(1) == 0)
    def _():
        ct_ref[...] = jnp.zeros_like(ct_ref)
        n_ref[...] = jnp.zeros_like(n_ref)
        m_ref[...] = jnp.zeros_like(m_ref)

    causal = _iota((C, C), 0) >= _iota((C, C), 1)
    tri = causal.astype(BF16)
    bd = (_iota((MLW, MLW), 0) // ML_DV) == (_iota((MLW, MLW), 1) // ML_DK)
    bd_bf = bd.astype(BF16)
    lo_bf = _lane_mask(LANES, 0, ML_DV)
    hi_bf = _lane_mask(LANES, ML_DV, LANES)
    lo = _iota((1, LANES), 1) < ML_DV
    widen = (_iota((LANES, MLW), 0) == (_iota((LANES, MLW), 1) // ML_DV + SM_F)).astype(BF16)
    ones_blk = jnp.ones((C, LANES), BF16)
    row_id = _iota((C, LANES), 0)

    n_chunks = T // C
    rows = [slice(c * C, (c + 1) * C) for c in range(n_chunks)]
    i_pre = [mi_ref[r, :] for r in rows]
    f_cum = [_cumsum_rows(tri, mf_ref[r, :]) for r in rows]
    qs = [q_ref[r, :] for r in rows]
    ks = [k_ref[r, :] for r in rows]
    raw = [[_dot_nt(qm, kk) for qm, kk in _qk_tiles(q, k, ML_DK)] for q, k in zip(qs, ks)]

    g_t, g_max, b_last, log_w, w_max = [], [], [], [], []
    for c in range(n_chunks):
        g = i_pre[c] - f_cum[c]
        g_t.append(g.T)
        run = g
        sft = 1
        while sft < C:
            run = jnp.maximum(run, jnp.where(row_id >= sft, pltpu.roll(run, sft, 0), -jnp.inf))
            sft *= 2
        g_max.append(run)
        b_last.append(f_cum[c][C - 1:C, :])
        log_w.append(b_last[c] - f_cum[c] + i_pre[c])
        w_max.append(jnp.max(log_w[c], axis=0, keepdims=True))
    ms = [m_ref[...]]
    for c in range(n_chunks):
        ms.append(jnp.maximum(b_last[c] + ms[c], w_max[c]))

    mm, wide = [], []
    for c in range(n_chunks):
        mm.append(jnp.maximum(g_max[c], ms[c]))
        slab = jnp.concatenate([jnp.exp(ms[c] - mm[c]), jnp.exp(-(f_cum[c] + mm[c])), jnp.exp(log_w[c] - ms[c + 1]),
                                jnp.broadcast_to(jnp.exp(b_last[c] + ms[c] - ms[c + 1]), (8, LANES))], axis=0)
        s_hi, s_lo = _split2(slab)
        wide.append(_dot(s_hi, widen) + _dot(s_lo, widen))
    w_inter = [w[0:C] for w in wide]
    e_negm = [w[C:2 * C] for w in wide]
    gf = [w[3 * C:3 * C + 1] for w in wide]
    wk = [w[2 * C:3 * C] * k.astype(F32) for w, k in zip(wide, ks)]
    upd = [_dot_tn(v_ref[r, :], x.astype(BF16)) for r, x in zip(rows, wk)]

    cts, ns = [ct_ref[...]], [n_ref[...]]
    for c in range(n_chunks):
        cts.append(cts[c] * gf[c] + jnp.where(bd, upd[c], 0.0))
        ns.append(ns[c] * gf[c] + jnp.sum(wk[c], axis=0, keepdims=True))
    inter = [_dot_nt(qs[c], jnp.concatenate(
        [cts[c].astype(BF16), jnp.broadcast_to(ns[c].astype(BF16), (MLW, MLW)) * bd_bf], axis=0))
        for c in range(n_chunks)]

    res = []
    for c in range(n_chunks):
        per_head = []
        for h in range(ML_HEADS):
            lane = SM_F + h
            dm = jnp.where(causal, jnp.exp(g_t[c][lane:lane + 1, :] - mm[c][:, lane:lane + 1]), 0.0)
            s = (raw[c][h] * dm).astype(BF16)
            vp = v_ref[rows[c], LANES * (h // 2):LANES * (h // 2 + 1)]
            v_ext = jnp.concatenate([vp * (lo_bf if h % 2 == 0 else hi_bf), ones_blk], axis=1)
            per_head.append(_dot(s, v_ext))
        res.append(per_head)
    for c in range(n_chunks):
        for p in range(ML_HEADS // 2):
            cols = slice(LANES * p, LANES * (p + 1))
            ra, rb = res[c][2 * p], res[c][2 * p + 1]
            num = ra[:, :LANES] + rb[:, :LANES] + w_inter[c][:, cols] * inter[c][:, cols]
            den = (jnp.where(lo, ra[:, LANES:], rb[:, LANES:])
                   + w_inter[c][:, cols] * inter[c][:, MLW + LANES * p:MLW + LANES * (p + 1)])
            h_t = num / jnp.maximum(jnp.abs(den), e_negm[c][:, cols])
            o_ref[rows[c], cols] = (og_ref[rows[c], cols].astype(F32) * h_t).astype(BF16)
    ct_ref[...] = cts[n_chunks]
    n_ref[...] = ns[n_chunks]
    m_ref[...] = ms[n_chunks]


def _mlstm(q, k, v, og, mi, mf, *, B, S, T=MIX_TILE, C=MIX_CHUNK):
    nt, row, full = _mix_specs(S, T)
    return pl.pallas_call(
        functools.partial(_mlstm_kernel, T=T, C=C),
        out_shape=jax.ShapeDtypeStruct((B * S, MLW), BF16),
        grid=(B, nt),
        in_specs=[row(MLW), row(MLW), row(MLW), row(MLW), row(SMALL_W), row(SMALL_W)],
        out_specs=row(MLW),
        scratch_shapes=[pltpu.VMEM((MLW, MLW), F32), pltpu.VMEM((1, MLW), F32), pltpu.VMEM((1, LANES), F32)],
        compiler_params=_params(("arbitrary", "arbitrary")),
        name="mlstm",
    )(q, k, v, og, mi, mf)


def _memkv_kernel(mem_ref, g_ref, w_ref, k_ref, v_ref):
    mn = _rms(mem_ref[0], g_ref[...]).astype(BF16)
    d = mem_ref.shape[-1]
    k_ref[0] = _dot(mn, w_ref[:, :d]).astype(BF16)
    v_ref[0] = _dot(mn, w_ref[:, d:]).astype(BF16)


def _memkv(mem, l, g, w_kv):
    b, m, d = mem.shape
    blk = pl.BlockSpec((1, m, d), lambda i: (i, 0, 0))
    return pl.pallas_call(
        _memkv_kernel,
        out_shape=(jax.ShapeDtypeStruct((b, m, d), BF16), jax.ShapeDtypeStruct((b, m, d), BF16)),
        grid=(b,),
        in_specs=[blk, _layer(l, (1, d), 1), _layer(l, (d, 2 * d), 1)],
        out_specs=(blk, blk),
        compiler_params=_params(("parallel",)),
        name="mem_kv",
    )(mem, g, w_kv)


def _outxa_kernel(x_ref, og_ref, or_ref, om_ref, wo_ref, g_ref, wq_ref, k_ref, v_ref, wxo_ref, o_ref):
    x1 = (x_ref[...] + _dot(og_ref[...], wo_ref[0:VW, :]) + _dot(or_ref[...], wo_ref[VW:2 * VW, :])
          + _dot(om_ref[...], wo_ref[2 * VW:, :]))
    hn = _rms(x1, g_ref[...]).astype(BF16)
    q = (_dot(hn, wq_ref[...]) * (XA_DH ** -0.5)).astype(BF16)
    outs = []
    for h in range(XA_HEADS):
        cols = slice(XA_DH * h, XA_DH * (h + 1))
        s = _dot_nt(q[:, cols], k_ref[0, :, cols])
        s = s - jnp.max(s, axis=-1, keepdims=True)
        e = jnp.exp(s)
        p = e / jnp.sum(e, axis=-1, keepdims=True)
        outs.append(_dot(p.astype(BF16), v_ref[0, :, cols]).astype(BF16))
    o = jnp.concatenate(outs, axis=-1)
    o_ref[...] = x1 + _dot(o, wxo_ref[...])


def _outxa(x2d, o_gla, o_ret, o_ml, l, w_out, g_xa, w_q, mem_k, mem_v, w_o, *, B, S, tm=512):
    n, d = x2d.shape
    nt = S // tm
    m = mem_k.shape[1]
    row = lambda w: pl.BlockSpec((tm, w), lambda b, i: (b * nt + i, 0))
    kv = pl.BlockSpec((1, m, d), lambda b, i: (b, 0, 0))
    return pl.pallas_call(
        _outxa_kernel,
        out_shape=jax.ShapeDtypeStruct((n, d), F32),
        grid=(B, nt),
        in_specs=[row(d), row(VW), row(VW), row(MLW), _layer(l, (d, d), 2), _layer(l, (1, d), 2),
                  _layer(l, (d, d), 2), kv, kv, _layer(l, (d, d), 2)],
        out_specs=row(d),
        compiler_params=_params(("parallel", "parallel")),
        name="outproj_xattn",
    )(x2d, o_gla, o_ret, o_ml, w_out, g_xa, w_q, mem_k, mem_v, w_o)


def kernel(x, mem, positions, g_ffa, w_ffa_gu, w_ffa_down, g_mix, w_in, gla_w_a2, gla_b_a, gla_g_norm,
           ret_g_norm, ml_conv, ml_b_i, ml_b_f, w_out, g_xa, g_mem, w_xa_q, w_xa_kv, w_xa_o, g_ffb,
           w_ffb_gu, w_ffb_down, g_final):
    B, S, D = x.shape
    L = w_in.shape[0]
    N = B * S
    bf = lambda t: t.astype(BF16)
    vec = lambda t: t[:, None, :]
    w_ffa_gu, w_ffa_down, w_ffb_gu, w_ffb_down = bf(w_ffa_gu), bf(w_ffa_down), bf(w_ffb_gu), bf(w_ffb_down)
    w_out_b, w_q_b, w_kv_b, w_o_b = bf(w_out), bf(w_xa_q), bf(w_xa_kv), bf(w_xa_o)
    w_cat = _build_w_cat(w_in)
    wa2p = bf(jnp.pad(gla_w_a2, ((0, 0), (0, SMALL_W - GLA_RANK), (0, 0))))
    ml_bias = jnp.pad(jnp.concatenate([ml_b_i, ml_b_f], axis=-1),
                      ((0, 0), (SM_I, SMALL_W - SM_I - 2 * ML_HEADS)))
    g_ffa, g_mix, g_xa, g_mem, g_ffb = vec(g_ffa), vec(g_mix), vec(g_xa), vec(g_mem), vec(g_ffb)
    gla_b_a, gla_g_norm, ret_g_norm, ml_bias = vec(gla_b_a), vec(gla_g_norm), vec(ret_g_norm), vec(ml_bias)
    g_final = g_final[None, :]

    cos_t, sin_t = _rope_tables(positions.reshape(1, N))
    h = x.reshape(N, D)
    for l in range(L):
        h = _ffn(h, l, g_ffa, w_ffa_gu, w_ffa_down, g_final, final=False)
        (gq, gk, gla, gv, gr, rq, rks, rkd, rv, rg, mq, mk, mv, mo, mi, mf) = _inproj(
            h, l, g_mix, w_cat, cos_t, sin_t, wa2p, gla_b_a, ml_conv, ml_bias, S=S)
        o_gla = _gla(gq, gk, gla, gv, gr, l, gla_g_norm, B=B, S=S)
        o_ret = _ret(rq, rks, rkd, rv, rg, l, ret_g_norm, B=B, S=S)
        o_ml = _mlstm(mq, mk, mv, mo, mi, mf, B=B, S=S)
        mem_k, mem_v = _memkv(mem, l, g_mem, w_kv_b)
        h = _outxa(h, o_gla, o_ret, o_ml, l, w_out_b, g_xa, w_q_b, mem_k, mem_v, w_o_b, B=B, S=S)
        h = _ffn(h, l, g_ffb, w_ffb_gu, w_ffb_down, g_final, final=(l == L - 1))
    return h.reshape(B, S, D)
```

```python
import functools

import numpy as np
import jax
import jax.numpy as jnp
from jax import lax
from jax.experimental import pallas as pl
from jax.experimental.pallas import tpu as pltpu

F32 = jnp.float32
BF16 = jnp.bfloat16
EPS = 1e-6

D_MODEL = 1024
GLA_HEADS, GLA_DK, GLA_DV, GLA_RANK, GLA_TAU = 6, 32, 64, 16, 16.0
RET_HEADS, RET_DK, RET_DV = 6, 32, 64
ML_HEADS, ML_DK, ML_DV, ML_CONV = 4, 64, 64, 4
ROPE_BASE = 10000.0
XA_HEADS = 4
XA_DH = D_MODEL // XA_HEADS
IN_SIZES = (
    GLA_HEADS * GLA_DK, GLA_HEADS * GLA_DK, GLA_HEADS * GLA_DV, GLA_RANK, GLA_HEADS * GLA_DV,
    RET_HEADS * RET_DK, RET_HEADS * RET_DK, RET_HEADS * RET_DV, RET_HEADS * RET_DV,
    ML_HEADS * ML_DK, ML_HEADS * ML_DK, ML_HEADS * ML_DV, ML_HEADS * ML_DV, ML_HEADS, ML_HEADS,
)

LANES = 128
QK = GLA_HEADS * GLA_DK
VW = GLA_HEADS * GLA_DV
MLW = ML_HEADS * ML_DK
SMALL_W = LANES
SM_I = GLA_RANK
SM_F = GLA_RANK + ML_HEADS

OFF_GQ, OFF_GK, OFF_GV, OFF_GR = 0, 256, 512, 896
OFF_RQ, OFF_RQS, OFF_RK, OFF_RKS, OFF_RV, OFF_RG = 1280, 1536, 1792, 2048, 2304, 2688
OFF_MQK, OFF_MV, OFF_MO, OFF_SM = 3072, 3584, 3840, 4096
W_CAT = 4224

MIX_CHUNK = 128
MIX_TILE = 1024
TOKEN_TILE = 1024
GLA_SAFE_LOG = 40.0
VMEM_LIMIT = 48 * 1024 * 1024


def _dot(a, b):
    return jnp.dot(a, b, preferred_element_type=F32)


def _dot_nt(a, b):
    return lax.dot_general(a, b, (((1,), (1,)), ((), ())), preferred_element_type=F32)


def _dot_tn(a, b):
    return lax.dot_general(a, b, (((0,), (0,)), ((), ())), preferred_element_type=F32)


def _sigmoid(x):
    return 1.0 / (1.0 + jnp.exp(-x))


def _silu(x):
    return x * _sigmoid(x)


def _log_sigmoid(x):
    return jnp.minimum(x, 0.0) - jnp.log1p(jnp.exp(-jnp.abs(x)))


def _rms(x, g):
    return x * lax.rsqrt(jnp.mean(x * x, axis=-1, keepdims=True) + EPS) * g


def _split2(x):
    hi = x.astype(BF16)
    return hi, (x - hi.astype(F32)).astype(BF16)


def _split3(x):
    hi = x.astype(BF16)
    r1 = x - hi.astype(F32)
    mid = r1.astype(BF16)
    return hi, mid, (r1 - mid.astype(F32)).astype(BF16)


def _cumsum_rows(tri_bf, x):
    hi, mid, lo = _split3(x)
    return _dot(tri_bf, hi) + _dot(tri_bf, mid) + _dot(tri_bf, lo)


def _iota(shape, dim):
    return lax.broadcasted_iota(jnp.int32, shape, dim)


def _lane_mask(width, lo, hi, dtype=BF16):
    lane = _iota((1, width), 1)
    return ((lane >= lo) & (lane < hi)).astype(dtype)


def _params(sem):
    return pltpu.CompilerParams(dimension_semantics=sem, vmem_limit_bytes=VMEM_LIMIT)


def _layer(l, tail, grid_rank):
    idx = (l,) + (0,) * len(tail)
    imap = (lambda i: idx) if grid_rank == 1 else (lambda b, i: idx)
    return pl.BlockSpec((None,) + tuple(tail), imap, pipeline_mode=pl.Buffered(1))


def _const(shape, grid_rank):
    idx = (0,) * len(shape)
    imap = (lambda i: idx) if grid_rank == 1 else (lambda b, i: idx)
    return pl.BlockSpec(tuple(shape), imap, pipeline_mode=pl.Buffered(1))


FFN_SUB = 256
FFN_CHUNK = 1024


def _ffn_kernel(x_ref, g_ref, wgu_ref, wd_ref, gf_ref, o_ref, *, final):
    tm = x_ref.shape[0]
    d_ff = wd_ref.shape[0]
    for r in range(tm // FFN_SUB):
        rows = slice(r * FFN_SUB, (r + 1) * FFN_SUB)
        x = x_ref[rows, :]
        hn = _rms(x, g_ref[...]).astype(BF16)
        acc = None
        for off in range(0, d_ff, FFN_CHUNK):
            cw = min(FFN_CHUNK, d_ff - off)
            a = _dot(hn, wgu_ref[:, off:off + cw])
            g = _dot(hn, wgu_ref[:, d_ff + off:d_ff + off + cw])
            part = _dot((_silu(a) * g).astype(BF16), wd_ref[off:off + cw, :])
            acc = part if acc is None else acc + part
        y = x + 0.5 * acc
        if final:
            y = _rms(y, gf_ref[...])
        o_ref[rows, :] = y


def _ffn(x2d, l, g, w_gu, w_down, g_final, *, final, tm=TOKEN_TILE):
    n, d = x2d.shape
    d_ff = w_down.shape[1]
    return pl.pallas_call(
        functools.partial(_ffn_kernel, final=final),
        out_shape=jax.ShapeDtypeStruct((n, d), F32),
        grid=(n // tm,),
        in_specs=[
            pl.BlockSpec((tm, d), lambda i: (i, 0)),
            _layer(l, (1, d), 1),
            _layer(l, (d, 2 * d_ff), 1),
            _layer(l, (d_ff, d), 1),
            _const((1, d), 1),
        ],
        out_specs=pl.BlockSpec((tm, d), lambda i: (i, 0)),
        compiler_params=_params(("parallel",)),
        name="ffn",
    )(x2d, g, w_gu, w_down, g_final)


def _rope_kernel(pos_ref, invf_ref, spread_ref, sgn_ref, cos_ref, sin_ref):
    ang = invf_ref[...] * pos_ref[...].astype(F32)
    spread = spread_ref[...]

    def to_rows(t):
        return sum(_dot_tn(part, spread) for part in _split3(t))

    cos_ref[...] = to_rows(jnp.cos(ang))
    sin_ref[...] = to_rows(jnp.sin(ang)) * sgn_ref[...]


def _rope_tables(pos_row, *, tm=512):
    n = pos_row.shape[1]
    half = RET_DK // 2
    inv_freq = 1.0 / (ROPE_BASE ** jnp.linspace(0.0, 1.0, half, dtype=F32))
    spread = (np.arange(half)[:, None] == (np.arange(QK)[None, :] % half)).astype(np.float32)
    sgn = np.tile(np.concatenate([-np.ones(half, np.float32), np.ones(half, np.float32)]), RET_HEADS)[None, :]
    return pl.pallas_call(
        _rope_kernel,
        out_shape=(jax.ShapeDtypeStruct((n, QK), F32), jax.ShapeDtypeStruct((n, QK), F32)),
        grid=(n // tm,),
        in_specs=[
            pl.BlockSpec((1, tm), lambda i: (0, i)),
            _const((half, 1), 1),
            _const((half, QK), 1),
            _const((1, QK), 1),
        ],
        out_specs=(pl.BlockSpec((tm, QK), lambda i: (i, 0)), pl.BlockSpec((tm, QK), lambda i: (i, 0))),
        compiler_params=_params(("parallel",)),
        name="rope_tables",
    )(pos_row, inv_freq[:, None], jnp.asarray(spread, BF16), jnp.asarray(sgn))


INPROJ_SUB = 256


def _ret_log_gamma(width, head_w):
    head = _iota((1, width), 1) // head_w
    lg = jnp.zeros((1, width), F32)
    for h in range(RET_HEADS):
        lg = jnp.where(head == h, float(np.log(1.0 - 2.0 ** (-5.0 - h))), lg)
    return lg


def _inproj_kernel(x_ref, g_ref, w_ref, cos_ref, sin_ref, wa2_ref, ba_ref, cw_ref, mb_ref,
                   gq_ref, gk_ref, gla_ref, gv_ref, gr_ref, rq_ref, rks_ref, rkd_ref, rv_ref, rg_ref,
                   mq_ref, mk_ref, mv_ref, mo_ref, mi_ref, mf_ref, cv_ref, *, tiles_per_seq, C):
    tm = x_ref.shape[0]
    sub = INPROJ_SUB
    pos_in_chunk = (_iota((sub, 1), 0) & (C - 1)).astype(F32)
    k_dec = jnp.exp(_ret_log_gamma(QK, RET_DK) * (C - 1.0 - pos_in_chunk))
    lane = _iota((1, SMALL_W), 1)
    gate = (lane >= SM_F) & (lane < SM_F + ML_HEADS)
    w = cw_ref[...]
    row8 = _iota((8, 2 * MLW), 0)

    @pl.when(pl.program_id(0) % tiles_per_seq == 0)
    def _():
        cv_ref[...] = jnp.zeros_like(cv_ref)

    prev = cv_ref[...]
    for r in range(tm // sub):
        rows = slice(r * sub, (r + 1) * sub)
        hn = _rms(x_ref[rows, :], g_ref[...]).astype(BF16)

        def seg(off, width):
            return _dot(hn, w_ref[:, off:off + width])

        gq_ref[rows, :] = seg(OFF_GQ, QK)
        gk_ref[rows, :] = seg(OFF_GK, QK)
        gv_ref[rows, :] = seg(OFF_GV, VW).astype(BF16)
        gr_ref[rows, :] = _silu(seg(OFF_GR, VW)).astype(BF16)
        sm = seg(OFF_SM, SMALL_W)
        z = _dot(sm.astype(BF16), wa2_ref[...]) + ba_ref[...]
        gla_ref[rows, :] = _log_sigmoid(z) * (1.0 / GLA_TAU)

        c = cos_ref[rows, :]
        s = sin_ref[rows, :]
        rq_ref[rows, :] = (seg(OFF_RQ, QK) * c + seg(OFF_RQS, QK) * s).astype(BF16)
        rk = (seg(OFF_RK, QK) * c + seg(OFF_RKS, QK) * s) * (RET_DK ** -0.5)
        rks_ref[rows, :] = rk.astype(BF16)
        rkd_ref[rows, :] = (rk * k_dec).astype(BF16)
        rv_ref[rows, :] = seg(OFF_RV, VW).astype(BF16)
        rg_ref[rows, :] = _silu(seg(OFF_RG, VW)).astype(BF16)

        x = seg(OFF_MQK, 2 * MLW)
        x8 = x[0:8, :]
        acc = x * w[ML_CONV - 1:ML_CONV, :]
        acc8 = x8 * w[ML_CONV - 1:ML_CONV, :]
        for sft in range(1, ML_CONV):
            tap = w[ML_CONV - 1 - sft:ML_CONV - sft, :]
            acc = acc + pltpu.roll(x, sft, 0) * tap
            acc8 = acc8 + jnp.where(row8 < sft, pltpu.roll(prev, sft, 0), pltpu.roll(x8, sft, 0)) * tap
        prev = x[sub - 8:sub, :]
        y = _silu(jnp.concatenate([acc8, acc[8:, :]], axis=0))
        mq_ref[rows, :] = y[:, :MLW].astype(BF16)
        mk_ref[rows, :] = (y[:, MLW:] * (ML_DK ** -0.5)).astype(BF16)
        mv_ref[rows, :] = seg(OFF_MV, MLW).astype(BF16)
        mo_ref[rows, :] = _sigmoid(seg(OFF_MO, MLW)).astype(BF16)
        pre = sm + mb_ref[...]
        mi_ref[rows, :] = jnp.where(gate, pltpu.roll(pre, SM_F - SM_I, 1), 0.0)
        mf_ref[rows, :] = jnp.where(gate, _log_sigmoid(pre), 0.0)
    cv_ref[...] = prev


def _inproj(x2d, l, g, w_cat, cos_t, sin_t, wa2p, ba, cw, mb, *, S, tm=TOKEN_TILE, C=MIX_CHUNK):
    n, d = x2d.shape
    widths = [(QK, F32), (QK, F32), (QK, F32), (VW, BF16), (VW, BF16),
              (QK, BF16), (QK, BF16), (QK, BF16), (VW, BF16), (VW, BF16),
              (MLW, BF16), (MLW, BF16), (MLW, BF16), (MLW, BF16), (SMALL_W, F32), (SMALL_W, F32)]
    row = lambda w: pl.BlockSpec((tm, w), lambda i: (i, 0))
    return pl.pallas_call(
        functools.partial(_inproj_kernel, tiles_per_seq=S // tm, C=C),
        out_shape=tuple(jax.ShapeDtypeStruct((n, w), dt) for w, dt in widths),
        grid=(n // tm,),
        in_specs=[row(d), _layer(l, (1, d), 1), _layer(l, (d, W_CAT), 1), row(QK), row(QK),
                  _layer(l, (SMALL_W, QK), 1), _layer(l, (1, QK), 1), _layer(l, (ML_CONV, 2 * MLW), 1),
                  _layer(l, (1, SMALL_W), 1)],
        out_specs=tuple(row(w) for w, _ in widths),
        scratch_shapes=[pltpu.VMEM((8, 2 * MLW), F32)],
        compiler_params=_params(("arbitrary",)),
        name="inproj",
    )(x2d, g, w_cat, cos_t, sin_t, wa2p, ba, cw, mb)


def _build_w_cat(w_in):
    L, d, _ = w_in.shape
    offs = np.concatenate([[0], np.cumsum(IN_SIZES)])
    col = lambda i: w_in[:, :, offs[i]:offs[i + 1]]
    (gq, gk, gv, ga, gr, rq, rk, rv, rg, mq, mk, mv, mo, mi, mf) = [col(i) for i in range(15)]

    def swap(t):
        half = RET_DK // 2
        return t.reshape(L, d, RET_HEADS, 2, half)[:, :, :, ::-1, :].reshape(L, d, QK)

    def pad(t, width):
        return jnp.pad(t, ((0, 0), (0, 0), (0, width - t.shape[-1])))

    small = jnp.concatenate([ga, mi, mf], axis=-1)
    pieces = [pad(gq, 256), pad(gk, 256), gv, gr, pad(rq, 256), pad(swap(rq), 256), pad(rk, 256),
              pad(swap(rk), 256), rv, rg, mq, mk, mv, mo, pad(small, SMALL_W)]
    w_cat = jnp.concatenate(pieces, axis=-1).astype(BF16)
    assert w_cat.shape[-1] == W_CAT
    return w_cat


def _pair_sum(x, lo):
    s_lo = jnp.sum(jnp.where(lo, x, 0.0), axis=-1, keepdims=True)
    s_hi = jnp.sum(jnp.where(lo, 0.0, x), axis=-1, keepdims=True)
    return jnp.where(lo, s_lo, s_hi)


def _qk_tiles(q_bf, k_bf, head_w):
    width = q_bf.shape[-1]
    out = []
    for h in range(width // head_w):
        t0 = (h * head_w) // LANES * LANES
        t1 = min(t0 + LANES, width)
        m = _lane_mask(t1 - t0, h * head_w - t0, (h + 1) * head_w - t0)
        out.append((q_bf[:, t0:t1] * m, k_bf[:, t0:t1]))
    return out


def _gla_kernel(q_ref, k_ref, la_ref, v_ref, r_ref, gn_ref, o_ref, st_ref, oacc_ref, *, T, C):
    @pl.when(pl.program_id(1) == 0)
    def _():
        st_ref[...] = jnp.zeros_like(st_ref)

    n_chunks = T // C
    causal = _iota((C, C), 0) >= _iota((C, C), 1)
    tri = causal.astype(BF16)
    bd = (_iota((VW, QK), 0) // GLA_DV) == (_iota((VW, QK), 1) // GLA_DK)
    lo_bf = _lane_mask(LANES, 0, GLA_DV)
    hi_bf = _lane_mask(LANES, GLA_DV, LANES)
    lo = _iota((1, LANES), 1) < GLA_DV
    scale = GLA_DK ** -0.5

    bl_min = jnp.sum(la_ref[0:C, :], axis=0, keepdims=True)
    for c in range(1, n_chunks):
        bl_min = jnp.minimum(bl_min, jnp.sum(la_ref[c * C:(c + 1) * C, :], axis=0, keepdims=True))
    safe = jnp.min(bl_min) > -GLA_SAFE_LOG

    def finish(o, rws, cols):
        ms = _pair_sum(o * o, lo) * (1.0 / GLA_DV)
        return (o * lax.rsqrt(ms + EPS) * gn_ref[:, cols] * r_ref[rws, cols].astype(F32)).astype(BF16)

    @pl.when(safe)
    def _():
        bcs = [_cumsum_rows(tri, la_ref[c * C:(c + 1) * C, :]) for c in range(n_chunks)]
        st = st_ref[...]
        for c in range(n_chunks):
            rows = slice(c * C, (c + 1) * C)
            bc = bcs[c]
            bl = bc[C - 1:C, :]
            k = k_ref[rows, :]
            qt = (q_ref[rows, :] * (scale * jnp.exp(bc))).astype(BF16)
            kt = (k * jnp.exp(-bc)).astype(BF16)
            kh = (k * jnp.exp(bl - bc)).astype(BF16)
            sc = [(_dot_nt(qm, kk).astype(BF16) * tri) for qm, kk in _qk_tiles(qt, kt, GLA_DK)]
            inter = _dot_nt(qt, st.astype(BF16))
            for p in range(GLA_HEADS // 2):
                cols = slice(LANES * p, LANES * (p + 1))
                vp = v_ref[rows, cols]
                o = _dot(sc[2 * p], vp * lo_bf) + _dot(sc[2 * p + 1], vp * hi_bf) + inter[:, cols]
                o_ref[rows, cols] = finish(o, rows, cols)
            st = st * jnp.exp(bl) + jnp.where(bd, _dot_tn(v_ref[rows, :], kh), 0.0)
        st_ref[...] = st

    @pl.when(jnp.logical_not(safe))
    def _():
        vb = v_ref[...]
        row_id = _iota((T, 2 * LANES), 0)

        def token(t, carry):
            a_t = jnp.exp(la_ref[pl.ds(t, 1), :])
            k_t = k_ref[pl.ds(t, 1), :]
            q_t = q_ref[pl.ds(t, 1), :] * scale
            v_col = _dot_tn(vb, (row_id == t).astype(BF16))[:, :QK]
            st = st_ref[...] * a_t + jnp.where(bd, v_col * k_t, 0.0)
            st_ref[...] = st
            q8 = jnp.broadcast_to(q_t, (8, QK)).astype(BF16)
            oacc_ref[pl.ds(t, 1), :] = _dot_nt(q8, st.astype(BF16))[0:1, :]
            return carry

        lax.fori_loop(0, T, token, 0)
        for p in range(GLA_HEADS // 2):
            cols = slice(LANES * p, LANES * (p + 1))
            o_ref[:, cols] = finish(oacc_ref[:, cols], slice(0, T), cols)


def _mix_specs(S, T):
    nt = S // T
    row = lambda w: pl.BlockSpec((T, w), lambda b, i: (b * nt + i, 0))
    full = lambda a, c: pl.BlockSpec((a, c), lambda b, i: (0, 0))
    return nt, row, full


def _gla(q, k, la, v, r, l, gn, *, B, S, T=MIX_TILE, C=MIX_CHUNK):
    nt, row, full = _mix_specs(S, T)
    return pl.pallas_call(
        functools.partial(_gla_kernel, T=T, C=C),
        out_shape=jax.ShapeDtypeStruct((B * S, VW), BF16),
        grid=(B, nt),
        in_specs=[row(QK), row(QK), row(QK), row(VW), row(VW), _layer(l, (1, VW), 2)],
        out_specs=row(VW),
        scratch_shapes=[pltpu.VMEM((VW, QK), F32), pltpu.VMEM((T, VW), F32)],
        compiler_params=_params(("arbitrary", "arbitrary")),
        name="gla",
    )(q, k, la, v, r, gn)


def _ret_kernel(q_ref, ks_ref, kd_ref, v_ref, rg_ref, gn_ref, o_ref, rt_ref, dec_ref, *, T, C):
    @pl.when(pl.program_id(1) == 0)
    def _():
        rt_ref[...] = jnp.zeros_like(rt_ref)
        rel = (_iota((C, C), 0) - _iota((C, C), 1)).astype(F32)
        for h in range(RET_HEADS):
            log_g = float(np.log(1.0 - 2.0 ** (-5.0 - h)))
            dec_ref[h] = jnp.where(rel >= 0.0, jnp.exp(log_g * jnp.maximum(rel, 0.0)), 0.0)

    lo_bf = _lane_mask(LANES, 0, RET_DV)
    hi_bf = _lane_mask(LANES, RET_DV, LANES)
    lo = _iota((1, LANES), 1) < RET_DV
    idx = _iota((C, 1), 0).astype(F32)
    q_dec = jnp.exp(_ret_log_gamma(VW, RET_DV) * (idx + 1.0))
    chunk_dec = jnp.exp(_ret_log_gamma(QK, RET_DK) * float(C))
    bd = (_iota((VW, QK), 0) // RET_DV) == (_iota((VW, QK), 1) // RET_DK)

    rt = rt_ref[...]
    for c in range(T // C):
        rows = slice(c * C, (c + 1) * C)
        q = q_ref[rows, :]
        sc = [(_dot_nt(qm, kk) * dec_ref[h]).astype(BF16)
              for h, (qm, kk) in enumerate(_qk_tiles(q, ks_ref[rows, :], RET_DK))]
        inter = _dot_nt(q, rt.astype(BF16)) * q_dec
        for p in range(RET_HEADS // 2):
            cols = slice(LANES * p, LANES * (p + 1))
            vp = v_ref[rows, cols]
            o = _dot(sc[2 * p], vp * lo_bf) + _dot(sc[2 * p + 1], vp * hi_bf) + inter[:, cols]
            mu = _pair_sum(o, lo) * (1.0 / RET_DV)
            xc = o - mu
            var = _pair_sum(xc * xc, lo) * (1.0 / RET_DV)
            y = xc * lax.rsqrt(var + EPS) * gn_ref[:, cols] * rg_ref[rows, cols].astype(F32)
            o_ref[rows, cols] = y.astype(BF16)
        rt = rt * chunk_dec + jnp.where(bd, _dot_tn(v_ref[rows, :], kd_ref[rows, :]), 0.0)
    rt_ref[...] = rt


def _ret(q, ks, kd, v, rg, l, gn, *, B, S, T=MIX_TILE, C=MIX_CHUNK):
    nt, row, full = _mix_specs(S, T)
    return pl.pallas_call(
        functools.partial(_ret_kernel, T=T, C=C),
        out_shape=jax.ShapeDtypeStruct((B * S, VW), BF16),
        grid=(B, nt),
        in_specs=[row(QK), row(QK), row(QK), row(VW), row(VW), _layer(l, (1, VW), 2)],
        out_specs=row(VW),
        scratch_shapes=[pltpu.VMEM((VW, QK), F32), pltpu.VMEM((RET_HEADS, C, C), F32)],
        compiler_params=_params(("arbitrary", "arbitrary")),
        name="retention",
    )(q, ks, kd, v, rg, gn)


def _mlstm_kernel(q_ref, k_ref, v_ref, og_ref, mi_ref, mf_ref, o_ref, ct_ref, n_ref, m_ref, *, T, C):
    @pl.when(pl.program_id(1) == 0)
    def _():
        ct_ref[...] = jnp.zeros_like(ct_ref)
        n_ref[...] = jnp.zeros_like(n_ref)
        m_ref[...] = jnp.zeros_like(m_ref)

    causal = _iota((C, C), 0) >= _iota((C, C), 1)
    tri = causal.astype(BF16)
    bd = (_iota((MLW, MLW), 0) // ML_DV) == (_iota((MLW, MLW), 1) // ML_DK)
    bd_bf = bd.astype(BF16)
    lo_bf = _lane_mask(LANES, 0, ML_DV)
    hi_bf = _lane_mask(LANES, ML_DV, LANES)
    lo = _iota((1, LANES), 1) < ML_DV
    widen = (_iota((LANES, MLW), 0) == (_iota((LANES, MLW), 1) // ML_DV + SM_F)).astype(BF16)
    ones_blk = jnp.ones((C, LANES), BF16)
    row_id = _iota((C, LANES), 0)

    n_chunks = T // C
    rows = [slice(c * C, (c + 1) * C) for c in range(n_chunks)]
    i_pre = [mi_ref[r, :] for r in rows]
    f_cum = [_cumsum_rows(tri, mf_ref[r, :]) for r in rows]
    qs = [q_ref[r, :] for r in rows]
    ks = [k_ref[r, :] for r in rows]
    raw = [[_dot_nt(qm, kk) for qm, kk in _qk_tiles(q, k, ML_DK)] for q, k in zip(qs, ks)]

    g_t, g_max, b_last, log_w, w_max = [], [], [], [], []
    for c in range(n_chunks):
        g = i_pre[c] - f_cum[c]
        g_t.append(g.T)
        run = g
        sft = 1
        while sft < C:
            run = jnp.maximum(run, jnp.where(row_id >= sft, pltpu.roll(run, sft, 0), -jnp.inf))
            sft *= 2
        g_max.append(run)
        b_last.append(f_cum[c][C - 1:C, :])
        log_w.append(b_last[c] - f_cum[c] + i_pre[c])
        w_max.append(jnp.max(log_w[c], axis=0, keepdims=True))
    ms = [m_ref[...]]
    for c in range(n_chunks):
        ms.append(jnp.maximum(b_last[c] + ms[c], w_max[c]))

    mm, wide = [], []
    for c in range(n_chunks):
        mm.append(jnp.maximum(g_max[c], ms[c]))
        slab = jnp.concatenate([jnp.exp(ms[c] - mm[c]), jnp.exp(-(f_cum[c] + mm[c])), jnp.exp(log_w[c] - ms[c + 1]),
                                jnp.broadcast_to(jnp.exp(b_last[c] + ms[c] - ms[c + 1]), (8, LANES))], axis=0)
        s_hi, s_lo = _split2(slab)
        wide.append(_dot(s_hi, widen) + _dot(s_lo, widen))
    w_inter = [w[0:C] for w in wide]
    e_negm = [w[C:2 * C] for w in wide]
    gf = [w[3 * C:3 * C + 1] for w in wide]
    wk = [w[2 * C:3 * C] * k.astype(F32) for w, k in zip(wide, ks)]
    upd = [_dot_tn(v_ref[r, :], x.astype(BF16)) for r, x in zip(rows, wk)]

    cts, ns = [ct_ref[...]], [n_ref[...]]
    for c in range(n_chunks):
        cts.append(cts[c] * gf[c] + jnp.where(bd, upd[c], 0.0))
        ns.append(ns[c] * gf[c] + jnp.sum(wk[c], axis=0, keepdims=True))
    inter = [_dot_nt(qs[c], jnp.concatenate(
        [cts[c].astype(BF16), jnp.broadcast_to(ns[c].astype(BF16), (MLW, MLW)) * bd_bf], axis=0))
        for c in range(n_chunks)]

    res = []
    for c in range(n_chunks):
        per_head = []
        for h in range(ML_HEADS):
            lane = SM_F + h
            dm = jnp.where(causal, jnp.exp(g_t[c][lane:lane + 1, :] - mm[c][:, lane:lane + 1]), 0.0)
            s = (raw[c][h] * dm).astype(BF16)
            vp = v_ref[rows[c], LANES * (h // 2):LANES * (h // 2 + 1)]
            v_ext = jnp.concatenate([vp * (lo_bf if h % 2 == 0 else hi_bf), ones_blk], axis=1)
            per_head.append(_dot(s, v_ext))
        res.append(per_head)
    for c in range(n_chunks):
        for p in range(ML_HEADS // 2):
            cols = slice(LANES * p, LANES * (p + 1))
            ra, rb = res[c][2 * p], res[c][2 * p + 1]
            num = ra[:, :LANES] + rb[:, :LANES] + w_inter[c][:, cols] * inter[c][:, cols]
            den = (jnp.where(lo, ra[:, LANES:], rb[:, LANES:])
                   + w_inter[c][:, cols] * inter[c][:, MLW + LANES * p:MLW + LANES * (p + 1)])
            h_t = num / jnp.maximum(jnp.abs(den), e_negm[c][:, cols])
            o_ref[rows[c], cols] = (og_ref[rows[c], cols].astype(F32) * h_t).astype(BF16)
    ct_ref[...] = cts[n_chunks]
    n_ref[...] = ns[n_chunks]
    m_ref[...] = ms[n_chunks]


def _mlstm(q, k, v, og, mi, mf, *, B, S, T=MIX_TILE, C=MIX_CHUNK):
    nt, row, full = _mix_specs(S, T)
    return pl.pallas_call(
        functools.partial(_mlstm_kernel, T=T, C=C),
        out_shape=jax.ShapeDtypeStruct((B * S, MLW), BF16),
        grid=(B, nt),
        in_specs=[row(MLW), row(MLW), row(MLW), row(MLW), row(SMALL_W), row(SMALL_W)],
        out_specs=row(MLW),
        scratch_shapes=[pltpu.VMEM((MLW, MLW), F32), pltpu.VMEM((1, MLW), F32), pltpu.VMEM((1, LANES), F32)],
        compiler_params=_params(("arbitrary", "arbitrary")),
        name="mlstm",
    )(q, k, v, og, mi, mf)


def _memkv_kernel(mem_ref, g_ref, w_ref, k_ref, v_ref):
    mn = _rms(mem_ref[0], g_ref[...]).astype(BF16)
    d = mem_ref.shape[-1]
    k_ref[0] = _dot(mn, w_ref[:, :d]).astype(BF16)
    v_ref[0] = _dot(mn, w_ref[:, d:]).astype(BF16)


def _memkv(mem, l, g, w_kv):
    b, m, d = mem.shape
    blk = pl.BlockSpec((1, m, d), lambda i: (i, 0, 0))
    return pl.pallas_call(
        _memkv_kernel,
        out_shape=(jax.ShapeDtypeStruct((b, m, d), BF16), jax.ShapeDtypeStruct((b, m, d), BF16)),
        grid=(b,),
        in_specs=[blk, _layer(l, (1, d), 1), _layer(l, (d, 2 * d), 1)],
        out_specs=(blk, blk),
        compiler_params=_params(("parallel",)),
        name="mem_kv",
    )(mem, g, w_kv)


OUTXA_SUB = 256


def _outxa_kernel(x_ref, og_ref, or_ref, om_ref, wo_ref, g_ref, wq_ref, k_ref, v_ref, wxo_ref, o_ref):
    subs = [slice(r * OUTXA_SUB, (r + 1) * OUTXA_SUB) for r in range(x_ref.shape[0] // OUTXA_SUB)]
    x1 = [x_ref[r, :] + _dot(og_ref[r, :], wo_ref[0:VW, :]) + _dot(or_ref[r, :], wo_ref[VW:2 * VW, :])
          + _dot(om_ref[r, :], wo_ref[2 * VW:, :]) for r in subs]
    hn = [_rms(t, g_ref[...]).astype(BF16) for t in x1]
    q = [(_dot(t, wq_ref[...]) * (XA_DH ** -0.5)).astype(BF16) for t in hn]
    heads = [slice(XA_DH * h, XA_DH * (h + 1)) for h in range(XA_HEADS)]
    s = [[_dot_nt(t[:, c], k_ref[0, :, c]) for c in heads] for t in q]
    p = []
    for per_head in s:
        e = [jnp.exp(t - jnp.max(t, axis=-1, keepdims=True)) for t in per_head]
        p.append([(t / jnp.sum(t, axis=-1, keepdims=True)).astype(BF16) for t in e])
    o = [jnp.concatenate([_dot(t, v_ref[0, :, c]).astype(BF16) for t, c in zip(per_head, heads)], axis=-1)
         for per_head in p]
    for r, t, u in zip(subs, x1, o):
        o_ref[r, :] = t + _dot(u, wxo_ref[...])


def _outxa(x2d, o_gla, o_ret, o_ml, l, w_out, g_xa, w_q, mem_k, mem_v, w_o, *, B, S, tm=TOKEN_TILE):
    n, d = x2d.shape
    nt = S // tm
    m = mem_k.shape[1]
    row = lambda w: pl.BlockSpec((tm, w), lambda b, i: (b * nt + i, 0))
    kv = pl.BlockSpec((1, m, d), lambda b, i: (b, 0, 0))
    return pl.pallas_call(
        _outxa_kernel,
        out_shape=jax.ShapeDtypeStruct((n, d), F32),
        grid=(B, nt),
        in_specs=[row(d), row(VW), row(VW), row(MLW), _layer(l, (d, d), 2), _layer(l, (1, d), 2),
                  _layer(l, (d, d), 2), kv, kv, _layer(l, (d, d), 2)],
        out_specs=row(d),
        compiler_params=_params(("parallel", "parallel")),
        name="outproj_xattn",
    )(x2d, o_gla, o_ret, o_ml, w_out, g_xa, w_q, mem_k, mem_v, w_o)


def kernel(x, mem, positions, g_ffa, w_ffa_gu, w_ffa_down, g_mix, w_in, gla_w_a2, gla_b_a, gla_g_norm,
           ret_g_norm, ml_conv, ml_b_i, ml_b_f, w_out, g_xa, g_mem, w_xa_q, w_xa_kv, w_xa_o, g_ffb,
           w_ffb_gu, w_ffb_down, g_final):
    B, S, D = x.shape
    L = w_in.shape[0]
    N = B * S
    bf = lambda t: t.astype(BF16)
    vec = lambda t: t[:, None, :]
    w_ffa_gu, w_ffa_down, w_ffb_gu, w_ffb_down = bf(w_ffa_gu), bf(w_ffa_down), bf(w_ffb_gu), bf(w_ffb_down)
    w_out_b, w_q_b, w_kv_b, w_o_b = bf(w_out), bf(w_xa_q), bf(w_xa_kv), bf(w_xa_o)
    w_cat = _build_w_cat(w_in)
    wa2p = bf(jnp.pad(gla_w_a2, ((0, 0), (0, SMALL_W - GLA_RANK), (0, 0))))
    ml_bias = jnp.pad(jnp.concatenate([ml_b_i, ml_b_f], axis=-1),
                      ((0, 0), (SM_I, SMALL_W - SM_I - 2 * ML_HEADS)))
    g_ffa, g_mix, g_xa, g_mem, g_ffb = vec(g_ffa), vec(g_mix), vec(g_xa), vec(g_mem), vec(g_ffb)
    gla_b_a, gla_g_norm, ret_g_norm, ml_bias = vec(gla_b_a), vec(gla_g_norm), vec(ret_g_norm), vec(ml_bias)
    g_final = g_final[None, :]

    cos_t, sin_t = _rope_tables(positions.reshape(1, N))
    h = x.reshape(N, D)
    for l in range(L):
        h = _ffn(h, l, g_ffa, w_ffa_gu, w_ffa_down, g_final, final=False)
        (gq, gk, gla, gv, gr, rq, rks, rkd, rv, rg, mq, mk, mv, mo, mi, mf) = _inproj(
            h, l, g_mix, w_cat, cos_t, sin_t, wa2p, gla_b_a, ml_conv, ml_bias, S=S)
        o_gla = _gla(gq, gk, gla, gv, gr, l, gla_g_norm, B=B, S=S)
        o_ret = _ret(rq, rks, rkd, rv, rg, l, ret_g_norm, B=B, S=S)
        o_ml = _mlstm(mq, mk, mv, mo, mi, mf, B=B, S=S)
        mem_k, mem_v = _memkv(mem, l, g_mem, w_kv_b)
        h = _outxa(h, o_gla, o_ret, o_ml, l, w_out_b, g_xa, w_q_b, mem_k, mem_v, w_o_b, B=B, S=S)
        h = _ffn(h, l, g_ffb, w_ffb_gu, w_ffb_down, g_final, final=(l == L - 1))
    return h.reshape(B, S, D)
```

```python
import functools

import numpy as np
import jax
import jax.numpy as jnp
from jax import lax
from jax.experimental import pallas as pl
from jax.experimental.pallas import tpu as pltpu

F32 = jnp.float32
BF16 = jnp.bfloat16
EPS = 1e-6

D_MODEL = 1024
GLA_HEADS, GLA_DK, GLA_DV, GLA_RANK, GLA_TAU = 6, 32, 64, 16, 16.0
RET_HEADS, RET_DK, RET_DV = 6, 32, 64
ML_HEADS, ML_DK, ML_DV, ML_CONV = 4, 64, 64, 4
ROPE_BASE = 10000.0
XA_HEADS = 4
XA_DH = D_MODEL // XA_HEADS
IN_SIZES = (
    GLA_HEADS * GLA_DK, GLA_HEADS * GLA_DK, GLA_HEADS * GLA_DV, GLA_RANK, GLA_HEADS * GLA_DV,
    RET_HEADS * RET_DK, RET_HEADS * RET_DK, RET_HEADS * RET_DV, RET_HEADS * RET_DV,
    ML_HEADS * ML_DK, ML_HEADS * ML_DK, ML_HEADS * ML_DV, ML_HEADS * ML_DV, ML_HEADS, ML_HEADS,
)

LANES = 128
QK = GLA_HEADS * GLA_DK
VW = GLA_HEADS * GLA_DV
MLW = ML_HEADS * ML_DK
SMALL_W = LANES
SM_I = GLA_RANK
SM_F = GLA_RANK + ML_HEADS

QK0 = 4 * GLA_DK
QK1 = QK - QK0
OFF_GQ, OFF_GK, OFF_G1, OFF_R1, OFF_RQ, OFF_RK = 0, 128, 256, 384, 512, 640
OFF_GV, OFF_GR, OFF_RV, OFF_RG = 768, 1152, 1536, 1920
OFF_MQK, OFF_MV, OFF_MO, OFF_SM = 2304, 2816, 3072, 3328
W_CAT = 3456
INPROJ_GROUPS = ((OFF_GQ, OFF_GV), (OFF_GV, OFF_RV), (OFF_RV, OFF_MQK), (OFF_MQK, OFF_MV), (OFF_MV, W_CAT))

MIX_CHUNK = 128
MIX_TILE = 1024
TOKEN_TILE = 1024
GLA_SAFE_LOG = 40.0
VMEM_LIMIT = 48 * 1024 * 1024


def _dot(a, b):
    return jnp.dot(a, b, preferred_element_type=F32)


def _dot_nt(a, b):
    return lax.dot_general(a, b, (((1,), (1,)), ((), ())), preferred_element_type=F32)


def _dot_tn(a, b):
    return lax.dot_general(a, b, (((0,), (0,)), ((), ())), preferred_element_type=F32)


def _sigmoid(x):
    return 1.0 / (1.0 + jnp.exp(-x))


def _silu(x):
    return x * _sigmoid(x)


def _log_sigmoid(x):
    return jnp.minimum(x, 0.0) - jnp.log1p(jnp.exp(-jnp.abs(x)))


def _rms(x, g):
    return x * lax.rsqrt(jnp.mean(x * x, axis=-1, keepdims=True) + EPS) * g


def _split2(x):
    hi = x.astype(BF16)
    return hi, (x - hi.astype(F32)).astype(BF16)


def _split3(x):
    hi = x.astype(BF16)
    r1 = x - hi.astype(F32)
    mid = r1.astype(BF16)
    return hi, mid, (r1 - mid.astype(F32)).astype(BF16)


def _cumsum_rows(tri_bf, x):
    hi, mid, lo = _split3(x)
    return _dot(tri_bf, hi) + _dot(tri_bf, mid) + _dot(tri_bf, lo)


def _iota(shape, dim):
    return lax.broadcasted_iota(jnp.int32, shape, dim)


def _lane_mask(width, lo, hi, dtype=BF16):
    lane = _iota((1, width), 1)
    return ((lane >= lo) & (lane < hi)).astype(dtype)


def _params(sem):
    return pltpu.CompilerParams(dimension_semantics=sem, vmem_limit_bytes=VMEM_LIMIT)


def _layer(l, tail, grid_rank):
    idx = (l,) + (0,) * len(tail)
    imap = (lambda i: idx) if grid_rank == 1 else (lambda b, i: idx)
    return pl.BlockSpec((None,) + tuple(tail), imap, pipeline_mode=pl.Buffered(1))


def _const(shape, grid_rank):
    idx = (0,) * len(shape)
    imap = (lambda i: idx) if grid_rank == 1 else (lambda b, i: idx)
    return pl.BlockSpec(tuple(shape), imap, pipeline_mode=pl.Buffered(1))


FFN_SUB = 256
FFN_CHUNK = 1024


def _ffn_kernel(x_ref, g_ref, wgu_ref, wd_ref, gf_ref, o_ref, *, final):
    tm = x_ref.shape[0]
    d_ff = wd_ref.shape[0]
    for r in range(tm // FFN_SUB):
        rows = slice(r * FFN_SUB, (r + 1) * FFN_SUB)
        x = x_ref[rows, :]
        hn = _rms(x, g_ref[...]).astype(BF16)
        acc = None
        for off in range(0, d_ff, FFN_CHUNK):
            cw = min(FFN_CHUNK, d_ff - off)
            a = _dot(hn, wgu_ref[:, off:off + cw])
            g = _dot(hn, wgu_ref[:, d_ff + off:d_ff + off + cw])
            part = _dot((_silu(a) * g).astype(BF16), wd_ref[off:off + cw, :])
            acc = part if acc is None else acc + part
        y = x + 0.5 * acc
        if final:
            y = _rms(y, gf_ref[...])
        o_ref[rows, :] = y


def _ffn(x2d, l, g, w_gu, w_down, g_final, *, final, tm=TOKEN_TILE):
    n, d = x2d.shape
    d_ff = w_down.shape[1]
    return pl.pallas_call(
        functools.partial(_ffn_kernel, final=final),
        out_shape=jax.ShapeDtypeStruct((n, d), F32),
        grid=(n // tm,),
        in_specs=[
            pl.BlockSpec((tm, d), lambda i: (i, 0)),
            _layer(l, (1, d), 1),
            _layer(l, (d, 2 * d_ff), 1),
            _layer(l, (d_ff, d), 1),
            _const((1, d), 1),
        ],
        out_specs=pl.BlockSpec((tm, d), lambda i: (i, 0)),
        compiler_params=_params(("parallel",)),
        name="ffn",
    )(x2d, g, w_gu, w_down, g_final)


def _rope_kernel(pos_ref, invf_ref, spread_ref, sgn_ref, cos_ref, sin_ref):
    ang = invf_ref[...] * pos_ref[...].astype(F32)
    spread = spread_ref[...]

    def to_rows(t):
        return sum(_dot_tn(part, spread) for part in _split3(t))

    cos_ref[...] = to_rows(jnp.cos(ang))
    sin_ref[...] = to_rows(jnp.sin(ang)) * sgn_ref[...]


def _rope_tables(pos_row, *, tm=2048):
    n = pos_row.shape[1]
    tm = min(tm, n)
    half = RET_DK // 2
    inv_freq = 1.0 / (ROPE_BASE ** jnp.linspace(0.0, 1.0, half, dtype=F32))
    spread = (np.arange(half)[:, None] == (np.arange(LANES)[None, :] % half)).astype(np.float32)
    sgn = np.where(np.arange(LANES) % RET_DK < half, -1.0, 1.0).astype(np.float32)[None, :]
    return pl.pallas_call(
        _rope_kernel,
        out_shape=(jax.ShapeDtypeStruct((n, LANES), F32), jax.ShapeDtypeStruct((n, LANES), F32)),
        grid=(n // tm,),
        in_specs=[
            pl.BlockSpec((1, tm), lambda i: (0, i)),
            _const((half, 1), 1),
            _const((half, LANES), 1),
            _const((1, LANES), 1),
        ],
        out_specs=(pl.BlockSpec((tm, LANES), lambda i: (i, 0)), pl.BlockSpec((tm, LANES), lambda i: (i, 0))),
        compiler_params=_params(("parallel",)),
        name="rope_tables",
    )(pos_row, inv_freq[:, None], jnp.asarray(spread, BF16), jnp.asarray(sgn))


INPROJ_SUB = 256


def _ret_log_gamma(width, head_w):
    head = _iota((1, width), 1) // head_w
    lg = jnp.zeros((1, width), F32)
    for h in range(RET_HEADS):
        lg = jnp.where(head == h, float(np.log(1.0 - 2.0 ** (-5.0 - h))), lg)
    return lg


def _inproj_kernel(x_ref, g_ref, w_ref, cos_ref, sin_ref, wa2_ref, ba_ref, cw_ref, mb_ref,
                   gq_ref, gk_ref, gla_ref, gv_ref, gr_ref, rq_ref, rks_ref, rkd_ref, rv_ref, rg_ref,
                   mq_ref, mk_ref, mv_ref, mo_ref, mi_ref, mf_ref, cv_ref, *, tiles_per_seq, C):
    tm = x_ref.shape[0]
    sub = INPROJ_SUB
    pos_in_chunk = (_iota((sub, 1), 0) & (C - 1)).astype(F32)
    k_dec = jnp.exp(_ret_log_gamma(QK, RET_DK) * (C - 1.0 - pos_in_chunk))
    lane = _iota((1, SMALL_W), 1)
    gate = (lane >= SM_F) & (lane < SM_F + ML_HEADS)
    low_half = (lane % RET_DK) < RET_DK // 2
    w = cw_ref[...]
    row8 = _iota((8, 2 * MLW), 0)

    @pl.when(pl.program_id(0) % tiles_per_seq == 0)
    def _():
        cv_ref[...] = jnp.zeros_like(cv_ref)

    prev = cv_ref[...]
    for r in range(tm // sub):
        rows = slice(r * sub, (r + 1) * sub)
        hn = _rms(x_ref[rows, :], g_ref[...]).astype(BF16)

        wide = [(a, _dot(hn, w_ref[:, a:b])) for a, b in INPROJ_GROUPS]

        def seg(off, width, wide=wide):
            a, t = [(a, t) for a, t in wide if a <= off][-1]
            return t[:, off - a:off - a + width]

        g1 = seg(OFF_G1, LANES)
        gq_ref[rows, 0:QK0] = seg(OFF_GQ, QK0)
        gq_ref[rows, QK0:QK] = g1[:, :QK1]
        gk_ref[rows, 0:QK0] = seg(OFF_GK, QK0)
        gk_ref[rows, QK0:QK] = pltpu.roll(g1, QK1, 1)[:, :QK1]
        gv_ref[rows, :] = seg(OFF_GV, VW).astype(BF16)
        gr_ref[rows, :] = _silu(seg(OFF_GR, VW)).astype(BF16)
        sm = seg(OFF_SM, SMALL_W)
        z = _dot(sm.astype(BF16), wa2_ref[...]) + ba_ref[...]
        gla_ref[rows, :] = _log_sigmoid(z) * (1.0 / GLA_TAU)

        c = cos_ref[rows, :]
        s = sin_ref[rows, :]

        def rot(t):
            swapped = jnp.where(low_half, pltpu.roll(t, LANES - RET_DK // 2, 1), pltpu.roll(t, RET_DK // 2, 1))
            return t * c + swapped * s

        r1 = rot(seg(OFF_R1, LANES))
        rq_ref[rows, 0:QK0] = rot(seg(OFF_RQ, QK0)).astype(BF16)
        rq_ref[rows, QK0:QK] = r1[:, :QK1].astype(BF16)
        rk0 = rot(seg(OFF_RK, QK0)) * (RET_DK ** -0.5)
        rk1 = pltpu.roll(r1, QK1, 1)[:, :QK1] * (RET_DK ** -0.5)
        rks_ref[rows, 0:QK0] = rk0.astype(BF16)
        rks_ref[rows, QK0:QK] = rk1.astype(BF16)
        rkd_ref[rows, 0:QK0] = (rk0 * k_dec[:, 0:QK0]).astype(BF16)
        rkd_ref[rows, QK0:QK] = (rk1 * k_dec[:, QK0:QK]).astype(BF16)
        rv_ref[rows, :] = seg(OFF_RV, VW).astype(BF16)
        rg_ref[rows, :] = _silu(seg(OFF_RG, VW)).astype(BF16)

        x = seg(OFF_MQK, 2 * MLW)
        x8 = x[0:8, :]
        acc = x * w[ML_CONV - 1:ML_CONV, :]
        acc8 = x8 * w[ML_CONV - 1:ML_CONV, :]
        for sft in range(1, ML_CONV):
            tap = w[ML_CONV - 1 - sft:ML_CONV - sft, :]
            acc = acc + pltpu.roll(x, sft, 0) * tap
            acc8 = acc8 + jnp.where(row8 < sft, pltpu.roll(prev, sft, 0), pltpu.roll(x8, sft, 0)) * tap
        prev = x[sub - 8:sub, :]
        y = _silu(jnp.concatenate([acc8, acc[8:, :]], axis=0))
        mq_ref[rows, :] = y[:, :MLW].astype(BF16)
        mk_ref[rows, :] = (y[:, MLW:] * (ML_DK ** -0.5)).astype(BF16)
        mv_ref[rows, :] = seg(OFF_MV, MLW).astype(BF16)
        mo_ref[rows, :] = _sigmoid(seg(OFF_MO, MLW)).astype(BF16)
        pre = sm + mb_ref[...]
        mi_ref[rows, :] = jnp.where(gate, pltpu.roll(pre, SM_F - SM_I, 1), 0.0)
        mf_ref[rows, :] = jnp.where(gate, _log_sigmoid(pre), 0.0)
    cv_ref[...] = prev


def _inproj(x2d, l, g, w_cat, cos_t, sin_t, wa2p, ba, cw, mb, *, S, tm=TOKEN_TILE, C=MIX_CHUNK):
    n, d = x2d.shape
    widths = [(QK, F32), (QK, F32), (QK, F32), (VW, BF16), (VW, BF16),
              (QK, BF16), (QK, BF16), (QK, BF16), (VW, BF16), (VW, BF16),
              (MLW, BF16), (MLW, BF16), (MLW, BF16), (MLW, BF16), (SMALL_W, F32), (SMALL_W, F32)]
    row = lambda w: pl.BlockSpec((tm, w), lambda i: (i, 0))
    return pl.pallas_call(
        functools.partial(_inproj_kernel, tiles_per_seq=S // tm, C=C),
        out_shape=tuple(jax.ShapeDtypeStruct((n, w), dt) for w, dt in widths),
        grid=(n // tm,),
        in_specs=[row(d), _layer(l, (1, d), 1), _layer(l, (d, W_CAT), 1), row(LANES), row(LANES),
                  _layer(l, (SMALL_W, QK), 1), _layer(l, (1, QK), 1), _layer(l, (ML_CONV, 2 * MLW), 1),
                  _layer(l, (1, SMALL_W), 1)],
        out_specs=tuple(row(w) for w, _ in widths),
        scratch_shapes=[pltpu.VMEM((8, 2 * MLW), F32)],
        compiler_params=_params(("arbitrary",)),
        name="inproj",
    )(x2d, g, w_cat, cos_t, sin_t, wa2p, ba, cw, mb)


def _build_w_cat(w_in):
    offs = np.concatenate([[0], np.cumsum(IN_SIZES)])
    col = lambda i: w_in[:, :, offs[i]:offs[i + 1]]
    (gq, gk, gv, ga, gr, rq, rk, rv, rg, mq, mk, mv, mo, mi, mf) = [col(i) for i in range(15)]
    small = jnp.concatenate([ga, mi, mf], axis=-1)
    small = jnp.pad(small, ((0, 0), (0, 0), (0, SMALL_W - small.shape[-1])))
    pieces = [gq[..., :QK0], gk[..., :QK0], gq[..., QK0:], gk[..., QK0:], rq[..., QK0:], rk[..., QK0:],
              rq[..., :QK0], rk[..., :QK0], gv, gr, rv, rg, mq, mk, mv, mo, small]
    w_cat = jnp.concatenate(pieces, axis=-1).astype(BF16)
    assert w_cat.shape[-1] == W_CAT
    return w_cat


def _pair_sum(x, lo):
    s_lo = jnp.sum(jnp.where(lo, x, 0.0), axis=-1, keepdims=True)
    s_hi = jnp.sum(jnp.where(lo, 0.0, x), axis=-1, keepdims=True)
    return jnp.where(lo, s_lo, s_hi)


def _pair_dot(s_a, s_b, v_a, v_b):
    return _dot(jnp.concatenate([s_a, s_b], axis=1), jnp.concatenate([v_a, v_b], axis=0))


def _qk_tiles(q_bf, k_bf, head_w):
    width = q_bf.shape[-1]
    out = []
    for h in range(width // head_w):
        t0 = (h * head_w) // LANES * LANES
        t1 = min(t0 + LANES, width)
        m = _lane_mask(t1 - t0, h * head_w - t0, (h + 1) * head_w - t0)
        out.append((q_bf[:, t0:t1] * m, k_bf[:, t0:t1]))
    return out


def _gla_kernel(q_ref, k_ref, la_ref, v_ref, r_ref, gn_ref, o_ref, st_ref, oacc_ref, *, T, C):
    @pl.when(pl.program_id(1) == 0)
    def _():
        st_ref[...] = jnp.zeros_like(st_ref)

    n_chunks = T // C
    causal = _iota((C, C), 0) >= _iota((C, C), 1)
    tri = causal.astype(BF16)
    bd = (_iota((VW, QK), 0) // GLA_DV) == (_iota((VW, QK), 1) // GLA_DK)
    lo_bf = _lane_mask(LANES, 0, GLA_DV)
    hi_bf = _lane_mask(LANES, GLA_DV, LANES)
    lo = _iota((1, LANES), 1) < GLA_DV
    scale = GLA_DK ** -0.5

    bl_min = jnp.sum(la_ref[0:C, :], axis=0, keepdims=True)
    for c in range(1, n_chunks):
        bl_min = jnp.minimum(bl_min, jnp.sum(la_ref[c * C:(c + 1) * C, :], axis=0, keepdims=True))
    safe = jnp.min(bl_min) > -GLA_SAFE_LOG

    def finish(o, rws, cols):
        ms = _pair_sum(o * o, lo) * (1.0 / GLA_DV)
        return (o * lax.rsqrt(ms + EPS) * gn_ref[:, cols] * r_ref[rws, cols].astype(F32)).astype(BF16)

    @pl.when(safe)
    def _():
        bcs = [_cumsum_rows(tri, la_ref[c * C:(c + 1) * C, :]) for c in range(n_chunks)]
        def front(c):
            rows = slice(c * C, (c + 1) * C)
            bc = bcs[c]
            bl = bc[C - 1:C, :]
            k = k_ref[rows, :]
            qt = (q_ref[rows, :] * (scale * jnp.exp(bc))).astype(BF16)
            kt = (k * jnp.exp(-bc)).astype(BF16)
            kh = (k * jnp.exp(bl - bc)).astype(BF16)
            raw = [_dot_nt(qm, kk) for qm, kk in _qk_tiles(qt, kt, GLA_DK)]
            return rows, bl, qt, raw, _dot_tn(v_ref[rows, :], kh)

        st = st_ref[...]
        nxt = front(0)
        for c in range(n_chunks):
            rows, bl, qt, raw, upd = nxt
            if c + 1 < n_chunks:
                nxt = front(c + 1)
            sc = [s.astype(BF16) * tri for s in raw]
            inter = _dot_nt(qt, st.astype(BF16))
            for p in range(GLA_HEADS // 2):
                cols = slice(LANES * p, LANES * (p + 1))
                vp = v_ref[rows, cols]
                o = _pair_dot(sc[2 * p], sc[2 * p + 1], vp * lo_bf, vp * hi_bf) + inter[:, cols]
                o_ref[rows, cols] = finish(o, rows, cols)
            st = st * jnp.exp(bl) + jnp.where(bd, upd, 0.0)
        st_ref[...] = st

    @pl.when(jnp.logical_not(safe))
    def _():
        vb = v_ref[...]
        row_id = _iota((T, 2 * LANES), 0)

        def token(t, carry):
            a_t = jnp.exp(la_ref[pl.ds(t, 1), :])
            k_t = k_ref[pl.ds(t, 1), :]
            q_t = q_ref[pl.ds(t, 1), :] * scale
            v_col = _dot_tn(vb, (row_id == t).astype(BF16))[:, :QK]
            st = st_ref[...] * a_t + jnp.where(bd, v_col * k_t, 0.0)
            st_ref[...] = st
            q8 = jnp.broadcast_to(q_t, (8, QK)).astype(BF16)
            oacc_ref[pl.ds(t, 1), :] = _dot_nt(q8, st.astype(BF16))[0:1, :]
            return carry

        lax.fori_loop(0, T, token, 0)
        for p in range(GLA_HEADS // 2):
            cols = slice(LANES * p, LANES * (p + 1))
            o_ref[:, cols] = finish(oacc_ref[:, cols], slice(0, T), cols)


def _mix_specs(S, T):
    nt = S // T
    row = lambda w: pl.BlockSpec((T, w), lambda b, i: (b * nt + i, 0))
    full = lambda a, c: pl.BlockSpec((a, c), lambda b, i: (0, 0))
    return nt, row, full


def _gla(q, k, la, v, r, l, gn, *, B, S, T=MIX_TILE, C=MIX_CHUNK):
    nt, row, full = _mix_specs(S, T)
    return pl.pallas_call(
        functools.partial(_gla_kernel, T=T, C=C),
        out_shape=jax.ShapeDtypeStruct((B * S, VW), BF16),
        grid=(B, nt),
        in_specs=[row(QK), row(QK), row(QK), row(VW), row(VW), _layer(l, (1, VW), 2)],
        out_specs=row(VW),
        scratch_shapes=[pltpu.VMEM((VW, QK), F32), pltpu.VMEM((T, VW), F32)],
        compiler_params=_params(("arbitrary", "arbitrary")),
        name="gla",
    )(q, k, la, v, r, gn)


def _ret_kernel(q_ref, ks_ref, kd_ref, v_ref, rg_ref, gn_ref, o_ref, rt_ref, dec_ref, *, T, C):
    @pl.when(pl.program_id(1) == 0)
    def _():
        rt_ref[...] = jnp.zeros_like(rt_ref)
        rel = (_iota((C, C), 0) - _iota((C, C), 1)).astype(F32)
        for h in range(RET_HEADS):
            log_g = float(np.log(1.0 - 2.0 ** (-5.0 - h)))
            dec_ref[h] = jnp.where(rel >= 0.0, jnp.exp(log_g * jnp.maximum(rel, 0.0)), 0.0)

    lo_bf = _lane_mask(LANES, 0, RET_DV)
    hi_bf = _lane_mask(LANES, RET_DV, LANES)
    lo = _iota((1, LANES), 1) < RET_DV
    idx = _iota((C, 1), 0).astype(F32)
    q_dec = jnp.exp(_ret_log_gamma(VW, RET_DV) * (idx + 1.0))
    chunk_dec = jnp.exp(_ret_log_gamma(QK, RET_DK) * float(C))
    bd = (_iota((VW, QK), 0) // RET_DV) == (_iota((VW, QK), 1) // RET_DK)

    def front(c):
        rows = slice(c * C, (c + 1) * C)
        q = q_ref[rows, :]
        raw = [_dot_nt(qm, kk) for qm, kk in _qk_tiles(q, ks_ref[rows, :], RET_DK)]
        return rows, q, raw, _dot_tn(v_ref[rows, :], kd_ref[rows, :])

    rt = rt_ref[...]
    n_chunks = T // C
    nxt = front(0)
    for c in range(n_chunks):
        rows, q, raw, upd = nxt
        if c + 1 < n_chunks:
            nxt = front(c + 1)
        sc = [(s * dec_ref[h]).astype(BF16) for h, s in enumerate(raw)]
        inter = _dot_nt(q, rt.astype(BF16)) * q_dec
        for p in range(RET_HEADS // 2):
            cols = slice(LANES * p, LANES * (p + 1))
            vp = v_ref[rows, cols]
            o = _pair_dot(sc[2 * p], sc[2 * p + 1], vp * lo_bf, vp * hi_bf) + inter[:, cols]
            mu = _pair_sum(o, lo) * (1.0 / RET_DV)
            xc = o - mu
            var = _pair_sum(xc * xc, lo) * (1.0 / RET_DV)
            y = xc * lax.rsqrt(var + EPS) * gn_ref[:, cols] * rg_ref[rows, cols].astype(F32)
            o_ref[rows, cols] = y.astype(BF16)
        rt = rt * chunk_dec + jnp.where(bd, upd, 0.0)
    rt_ref[...] = rt


def _ret(q, ks, kd, v, rg, l, gn, *, B, S, T=MIX_TILE, C=MIX_CHUNK):
    nt, row, full = _mix_specs(S, T)
    return pl.pallas_call(
        functools.partial(_ret_kernel, T=T, C=C),
        out_shape=jax.ShapeDtypeStruct((B * S, VW), BF16),
        grid=(B, nt),
        in_specs=[row(QK), row(QK), row(QK), row(VW), row(VW), _layer(l, (1, VW), 2)],
        out_specs=row(VW),
        scratch_shapes=[pltpu.VMEM((VW, QK), F32), pltpu.VMEM((RET_HEADS, C, C), F32)],
        compiler_params=_params(("arbitrary", "arbitrary")),
        name="retention",
    )(q, ks, kd, v, rg, gn)


def _mlstm_kernel(q_ref, k_ref, v_ref, og_ref, mi_ref, mf_ref, o_ref, ct_ref, n_ref, m_ref, *, T, C):
    @pl.when(pl.program_id(1) == 0)
    def _():
        ct_ref[...] = jnp.zeros_like(ct_ref)
        n_ref[...] = jnp.zeros_like(n_ref)
        m_ref[...] = jnp.zeros_like(m_ref)

    causal = _iota((C, C), 0) >= _iota((C, C), 1)
    tri = causal.astype(BF16)
    bd = (_iota((MLW, MLW), 0) // ML_DV) == (_iota((MLW, MLW), 1) // ML_DK)
    bd_bf = bd.astype(BF16)
    lo_bf = _lane_mask(LANES, 0, ML_DV)
    hi_bf = _lane_mask(LANES, ML_DV, LANES)
    lo = _iota((1, LANES), 1) < ML_DV
    widen = (_iota((LANES, MLW), 0) == (_iota((LANES, MLW), 1) // ML_DV + SM_F)).astype(BF16)
    ones_blk = jnp.ones((C, LANES), BF16)
    row_id = _iota((C, LANES), 0)

    n_chunks = T // C
    rows = [slice(c * C, (c + 1) * C) for c in range(n_chunks)]
    i_pre = [mi_ref[r, :] for r in rows]
    f_cum = [_cumsum_rows(tri, mf_ref[r, :]) for r in rows]
    qs = [q_ref[r, :] for r in rows]
    ks = [k_ref[r, :] for r in rows]
    raw = [[_dot_nt(qm, kk) for qm, kk in _qk_tiles(q, k, ML_DK)] for q, k in zip(qs, ks)]

    g_t, g_max, b_last, log_w, w_max = [], [], [], [], []
    for c in range(n_chunks):
        g = i_pre[c] - f_cum[c]
        g_t.append(g.T)
        run = g
        sft = 1
        while sft < C:
            run = jnp.maximum(run, jnp.where(row_id >= sft, pltpu.roll(run, sft, 0), -jnp.inf))
            sft *= 2
        g_max.append(run)
        b_last.append(f_cum[c][C - 1:C, :])
        log_w.append(b_last[c] - f_cum[c] + i_pre[c])
        w_max.append(jnp.max(log_w[c], axis=0, keepdims=True))
    ms = [m_ref[...]]
    for c in range(n_chunks):
        ms.append(jnp.maximum(b_last[c] + ms[c], w_max[c]))

    mm, wide = [], []
    for c in range(n_chunks):
        mm.append(jnp.maximum(g_max[c], ms[c]))
        slab = jnp.concatenate([jnp.exp(ms[c] - mm[c]), jnp.exp(-(f_cum[c] + mm[c])), jnp.exp(log_w[c] - ms[c + 1]),
                                jnp.broadcast_to(jnp.exp(b_last[c] + ms[c] - ms[c + 1]), (8, LANES))], axis=0)
        s_hi, s_lo = _split2(slab)
        wide.append(_dot(s_hi, widen) + _dot(s_lo, widen))
    w_inter = [w[0:C] for w in wide]
    e_negm = [w[C:2 * C] for w in wide]
    gf = [w[3 * C:3 * C + 1] for w in wide]
    wk = [w[2 * C:3 * C] * k.astype(F32) for w, k in zip(wide, ks)]
    upd = [_dot_tn(v_ref[r, :], x.astype(BF16)) for r, x in zip(rows, wk)]

    cts, ns = [ct_ref[...]], [n_ref[...]]
    for c in range(n_chunks):
        cts.append(cts[c] * gf[c] + jnp.where(bd, upd[c], 0.0))
        ns.append(ns[c] * gf[c] + jnp.sum(wk[c], axis=0, keepdims=True))
    inter = [_dot_nt(qs[c], jnp.concatenate(
        [cts[c].astype(BF16), jnp.broadcast_to(ns[c].astype(BF16), (MLW, MLW)) * bd_bf], axis=0))
        for c in range(n_chunks)]

    res = []
    for c in range(n_chunks):
        per_head = []
        for h in range(ML_HEADS):
            lane = SM_F + h
            dm = jnp.where(causal, jnp.exp(g_t[c][lane:lane + 1, :] - mm[c][:, lane:lane + 1]), 0.0)
            s = (raw[c][h] * dm).astype(BF16)
            vp = v_ref[rows[c], LANES * (h // 2):LANES * (h // 2 + 1)]
            v_ext = jnp.concatenate([vp * (lo_bf if h % 2 == 0 else hi_bf), ones_blk], axis=1)
            per_head.append(_dot(s, v_ext))
        res.append(per_head)
    for c in range(n_chunks):
        for p in range(ML_HEADS // 2):
            cols = slice(LANES * p, LANES * (p + 1))
            ra, rb = res[c][2 * p], res[c][2 * p + 1]
            num = ra[:, :LANES] + rb[:, :LANES] + w_inter[c][:, cols] * inter[c][:, cols]
            den = (jnp.where(lo, ra[:, LANES:], rb[:, LANES:])
                   + w_inter[c][:, cols] * inter[c][:, MLW + LANES * p:MLW + LANES * (p + 1)])
            h_t = num / jnp.maximum(jnp.abs(den), e_negm[c][:, cols])
            o_ref[rows[c], cols] = (og_ref[rows[c], cols].astype(F32) * h_t).astype(BF16)
    ct_ref[...] = cts[n_chunks]
    n_ref[...] = ns[n_chunks]
    m_ref[...] = ms[n_chunks]


def _mlstm(q, k, v, og, mi, mf, *, B, S, T=MIX_TILE, C=MIX_CHUNK):
    nt, row, full = _mix_specs(S, T)
    return pl.pallas_call(
        functools.partial(_mlstm_kernel, T=T, C=C),
        out_shape=jax.ShapeDtypeStruct((B * S, MLW), BF16),
        grid=(B, nt),
        in_specs=[row(MLW), row(MLW), row(MLW), row(MLW), row(SMALL_W), row(SMALL_W)],
        out_specs=row(MLW),
        scratch_shapes=[pltpu.VMEM((MLW, MLW), F32), pltpu.VMEM((1, MLW), F32), pltpu.VMEM((1, LANES), F32)],
        compiler_params=_params(("arbitrary", "arbitrary")),
        name="mlstm",
    )(q, k, v, og, mi, mf)


def _memkv_kernel(mem_ref, g_ref, w_ref, k_ref, v_ref):
    mn = _rms(mem_ref[0], g_ref[...]).astype(BF16)
    d = mem_ref.shape[-1]
    k_ref[0] = _dot(mn, w_ref[:, :d]).astype(BF16)
    v_ref[0] = _dot(mn, w_ref[:, d:]).astype(BF16)


def _memkv(mem, l, g, w_kv):
    b, m, d = mem.shape
    blk = pl.BlockSpec((1, m, d), lambda i: (i, 0, 0))
    return pl.pallas_call(
        _memkv_kernel,
        out_shape=(jax.ShapeDtypeStruct((b, m, d), BF16), jax.ShapeDtypeStruct((b, m, d), BF16)),
        grid=(b,),
        in_specs=[blk, _layer(l, (1, d), 1), _layer(l, (d, 2 * d), 1)],
        out_specs=(blk, blk),
        compiler_params=_params(("parallel",)),
        name="mem_kv",
    )(mem, g, w_kv)


OUTXA_SUB = 256


def _outxa_kernel(x_ref, og_ref, or_ref, om_ref, wo_ref, g_ref, wq_ref, k_ref, v_ref, wxo_ref, o_ref):
    subs = [slice(r * OUTXA_SUB, (r + 1) * OUTXA_SUB) for r in range(x_ref.shape[0] // OUTXA_SUB)]
    x1 = [x_ref[r, :] + _dot(jnp.concatenate([og_ref[r, :], or_ref[r, :], om_ref[r, :]], axis=-1), wo_ref[...])
          for r in subs]
    hn = [_rms(t, g_ref[...]).astype(BF16) for t in x1]
    q = [(_dot(t, wq_ref[...]) * (XA_DH ** -0.5)).astype(BF16) for t in hn]
    heads = [slice(XA_DH * h, XA_DH * (h + 1)) for h in range(XA_HEADS)]
    s = [[_dot_nt(t[:, c], k_ref[0, :, c]) for c in heads] for t in q]
    p = []
    for per_head in s:
        e = [jnp.exp(t - jnp.max(t, axis=-1, keepdims=True)) for t in per_head]
        p.append([(t / jnp.sum(t, axis=-1, keepdims=True)).astype(BF16) for t in e])
    o = [jnp.concatenate([_dot(t, v_ref[0, :, c]).astype(BF16) for t, c in zip(per_head, heads)], axis=-1)
         for per_head in p]
    for r, t, u in zip(subs, x1, o):
        o_ref[r, :] = t + _dot(u, wxo_ref[...])


def _outxa(x2d, o_gla, o_ret, o_ml, l, w_out, g_xa, w_q, mem_k, mem_v, w_o, *, B, S, tm=TOKEN_TILE):
    n, d = x2d.shape
    nt = S // tm
    m = mem_k.shape[1]
    row = lambda w: pl.BlockSpec((tm, w), lambda b, i: (b * nt + i, 0))
    kv = pl.BlockSpec((1, m, d), lambda b, i: (b, 0, 0))
    return pl.pallas_call(
        _outxa_kernel,
        out_shape=jax.ShapeDtypeStruct((n, d), F32),
        grid=(B, nt),
        in_specs=[row(d), row(VW), row(VW), row(MLW), _layer(l, (d, d), 2), _layer(l, (1, d), 2),
                  _layer(l, (d, d), 2), kv, kv, _layer(l, (d, d), 2)],
        out_specs=row(d),
        compiler_params=_params(("parallel", "parallel")),
        name="outproj_xattn",
    )(x2d, o_gla, o_ret, o_ml, w_out, g_xa, w_q, mem_k, mem_v, w_o)


def kernel(x, mem, positions, g_ffa, w_ffa_gu, w_ffa_down, g_mix, w_in, gla_w_a2, gla_b_a, gla_g_norm,
           ret_g_norm, ml_conv, ml_b_i, ml_b_f, w_out, g_xa, g_mem, w_xa_q, w_xa_kv, w_xa_o, g_ffb,
           w_ffb_gu, w_ffb_down, g_final):
    B, S, D = x.shape
    L = w_in.shape[0]
    N = B * S
    bf = lambda t: t.astype(BF16)
    vec = lambda t: t[:, None, :]
    w_ffa_gu, w_ffa_down, w_ffb_gu, w_ffb_down = bf(w_ffa_gu), bf(w_ffa_down), bf(w_ffb_gu), bf(w_ffb_down)
    w_out_b, w_q_b, w_kv_b, w_o_b = bf(w_out), bf(w_xa_q), bf(w_xa_kv), bf(w_xa_o)
    w_cat = _build_w_cat(w_in)
    wa2p = bf(jnp.pad(gla_w_a2, ((0, 0), (0, SMALL_W - GLA_RANK), (0, 0))))
    ml_bias = jnp.pad(jnp.concatenate([ml_b_i, ml_b_f], axis=-1),
                      ((0, 0), (SM_I, SMALL_W - SM_I - 2 * ML_HEADS)))
    g_ffa, g_mix, g_xa, g_mem, g_ffb = vec(g_ffa), vec(g_mix), vec(g_xa), vec(g_mem), vec(g_ffb)
    gla_b_a, gla_g_norm, ret_g_norm, ml_bias = vec(gla_b_a), vec(gla_g_norm), vec(ret_g_norm), vec(ml_bias)
    g_final = g_final[None, :]

    cos_t, sin_t = _rope_tables(positions.reshape(1, N))
    h = x.reshape(N, D)
    for l in range(L):
        h = _ffn(h, l, g_ffa, w_ffa_gu, w_ffa_down, g_final, final=False)
        (gq, gk, gla, gv, gr, rq, rks, rkd, rv, rg, mq, mk, mv, mo, mi, mf) = _inproj(
            h, l, g_mix, w_cat, cos_t, sin_t, wa2p, gla_b_a, ml_conv, ml_bias, S=S)
        o_gla = _gla(gq, gk, gla, gv, gr, l, gla_g_norm, B=B, S=S)
        o_ret = _ret(rq, rks, rkd, rv, rg, l, ret_g_norm, B=B, S=S)
        o_ml = _mlstm(mq, mk, mv, mo, mi, mf, B=B, S=S)
        mem_k, mem_v = _memkv(mem, l, g_mem, w_kv_b)
        h = _outxa(h, o_gla, o_ret, o_ml, l, w_out_b, g_xa, w_q_b, mem_k, mem_v, w_o_b, B=B, S=S)
        h = _ffn(h, l, g_ffb, w_ffb_gu, w_ffb_down, g_final, final=(l == L - 1))
    return h.reshape(B, S, D)
```

```python
import functools

import numpy as np
import jax
import jax.numpy as jnp
from jax import lax
from jax.experimental import pallas as pl
from jax.experimental.pallas import tpu as pltpu

F32 = jnp.float32
BF16 = jnp.bfloat16
EPS = 1e-6

D_MODEL = 1024
GLA_HEADS, GLA_DK, GLA_DV, GLA_RANK, GLA_TAU = 6, 32, 64, 16, 16.0
RET_HEADS, RET_DK, RET_DV = 6, 32, 64
ML_HEADS, ML_DK, ML_DV, ML_CONV = 4, 64, 64, 4
ROPE_BASE = 10000.0
XA_HEADS = 4
XA_DH = D_MODEL // XA_HEADS
IN_SIZES = (
    GLA_HEADS * GLA_DK, GLA_HEADS * GLA_DK, GLA_HEADS * GLA_DV, GLA_RANK, GLA_HEADS * GLA_DV,
    RET_HEADS * RET_DK, RET_HEADS * RET_DK, RET_HEADS * RET_DV, RET_HEADS * RET_DV,
    ML_HEADS * ML_DK, ML_HEADS * ML_DK, ML_HEADS * ML_DV, ML_HEADS * ML_DV, ML_HEADS, ML_HEADS,
)

LANES = 128
QK = GLA_HEADS * GLA_DK
VW = GLA_HEADS * GLA_DV
MLW = ML_HEADS * ML_DK
SMALL_W = LANES
SM_I = GLA_RANK
SM_F = GLA_RANK + ML_HEADS

QK0 = 4 * GLA_DK
QK1 = QK - QK0
OFF_GQ, OFF_GK, OFF_G1, OFF_R1, OFF_RQ, OFF_RK = 0, 128, 256, 384, 512, 640
OFF_GV, OFF_GR, OFF_RV, OFF_RG = 768, 1152, 1536, 1920
OFF_MQK, OFF_MV, OFF_MO, OFF_SM = 2304, 2816, 3072, 3328
W_CAT = 3456
INPROJ_GROUPS = ((OFF_GQ, OFF_GV), (OFF_GV, OFF_RV), (OFF_RV, OFF_MQK), (OFF_MQK, OFF_MV), (OFF_MV, W_CAT))

MIX_CHUNK = 128
MIX_TILE = 1024
MIX_TILE_LONG = 2048
TOKEN_TILE = 1024
GLA_SAFE_LOG = 40.0
VMEM_LIMIT = 48 * 1024 * 1024


def _dot(a, b):
    return jnp.dot(a, b, preferred_element_type=F32)


def _dot_nt(a, b):
    return lax.dot_general(a, b, (((1,), (1,)), ((), ())), preferred_element_type=F32)


def _dot_tn(a, b):
    return lax.dot_general(a, b, (((0,), (0,)), ((), ())), preferred_element_type=F32)


def _sigmoid(x):
    return 1.0 / (1.0 + jnp.exp(-x))


def _silu(x):
    return x * _sigmoid(x)


def _log_sigmoid(x):
    return jnp.minimum(x, 0.0) - jnp.log1p(jnp.exp(-jnp.abs(x)))


def _rms(x, g):
    return x * lax.rsqrt(jnp.mean(x * x, axis=-1, keepdims=True) + EPS) * g


def _split2(x):
    hi = x.astype(BF16)
    return hi, (x - hi.astype(F32)).astype(BF16)


def _split3(x):
    hi = x.astype(BF16)
    r1 = x - hi.astype(F32)
    mid = r1.astype(BF16)
    return hi, mid, (r1 - mid.astype(F32)).astype(BF16)


def _cumsum_rows(tri_bf, x):
    hi, mid, lo = _split3(x)
    return _dot(tri_bf, hi) + _dot(tri_bf, mid) + _dot(tri_bf, lo)


def _iota(shape, dim):
    return lax.broadcasted_iota(jnp.int32, shape, dim)


def _lane_mask(width, lo, hi, dtype=BF16):
    lane = _iota((1, width), 1)
    return ((lane >= lo) & (lane < hi)).astype(dtype)


def _params(sem):
    return pltpu.CompilerParams(dimension_semantics=sem, vmem_limit_bytes=VMEM_LIMIT)


def _layer(l, tail, grid_rank):
    idx = (l,) + (0,) * len(tail)
    imap = (lambda i: idx) if grid_rank == 1 else (lambda b, i: idx)
    return pl.BlockSpec((None,) + tuple(tail), imap, pipeline_mode=pl.Buffered(1))


def _const(shape, grid_rank):
    idx = (0,) * len(shape)
    imap = (lambda i: idx) if grid_rank == 1 else (lambda b, i: idx)
    return pl.BlockSpec(tuple(shape), imap, pipeline_mode=pl.Buffered(1))


FFN_SUB = 256
FFN_CHUNK = 1024


def _ffn_kernel(x_ref, g_ref, wgu_ref, wd_ref, gf_ref, o_ref, *, final):
    tm = x_ref.shape[0]
    d_ff = wd_ref.shape[0]
    for r in range(tm // FFN_SUB):
        rows = slice(r * FFN_SUB, (r + 1) * FFN_SUB)
        x = x_ref[rows, :]
        hn = _rms(x, g_ref[...]).astype(BF16)
        acc = None
        for off in range(0, d_ff, FFN_CHUNK):
            cw = min(FFN_CHUNK, d_ff - off)
            a = _dot(hn, wgu_ref[:, off:off + cw])
            g = _dot(hn, wgu_ref[:, d_ff + off:d_ff + off + cw])
            part = _dot((_silu(a) * g).astype(BF16), wd_ref[off:off + cw, :])
            acc = part if acc is None else acc + part
        y = x + 0.5 * acc
        if final:
            y = _rms(y, gf_ref[...])
        o_ref[rows, :] = y


def _ffn(x2d, l, g, w_gu, w_down, g_final, *, final, tm=TOKEN_TILE):
    n, d = x2d.shape
    d_ff = w_down.shape[1]
    return pl.pallas_call(
        functools.partial(_ffn_kernel, final=final),
        out_shape=jax.ShapeDtypeStruct((n, d), F32),
        grid=(n // tm,),
        in_specs=[
            pl.BlockSpec((tm, d), lambda i: (i, 0)),
            _layer(l, (1, d), 1),
            _layer(l, (d, 2 * d_ff), 1),
            _layer(l, (d_ff, d), 1),
            _const((1, d), 1),
        ],
        out_specs=pl.BlockSpec((tm, d), lambda i: (i, 0)),
        compiler_params=_params(("parallel",)),
        name="ffn",
    )(x2d, g, w_gu, w_down, g_final)


def _rope_kernel(pos_ref, invf_ref, spread_ref, sgn_ref, cos_ref, sin_ref):
    ang = invf_ref[...] * pos_ref[...].astype(F32)
    spread = spread_ref[...]

    def to_rows(t):
        return sum(_dot_tn(part, spread) for part in _split3(t))

    cos_ref[...] = to_rows(jnp.cos(ang))
    sin_ref[...] = to_rows(jnp.sin(ang)) * sgn_ref[...]


def _rope_tables(pos_row, *, tm=2048):
    n = pos_row.shape[1]
    tm = min(tm, n)
    half = RET_DK // 2
    inv_freq = 1.0 / (ROPE_BASE ** jnp.linspace(0.0, 1.0, half, dtype=F32))
    spread = (np.arange(half)[:, None] == (np.arange(LANES)[None, :] % half)).astype(np.float32)
    sgn = np.where(np.arange(LANES) % RET_DK < half, -1.0, 1.0).astype(np.float32)[None, :]
    return pl.pallas_call(
        _rope_kernel,
        out_shape=(jax.ShapeDtypeStruct((n, LANES), F32), jax.ShapeDtypeStruct((n, LANES), F32)),
        grid=(n // tm,),
        in_specs=[
            pl.BlockSpec((1, tm), lambda i: (0, i)),
            _const((half, 1), 1),
            _const((half, LANES), 1),
            _const((1, LANES), 1),
        ],
        out_specs=(pl.BlockSpec((tm, LANES), lambda i: (i, 0)), pl.BlockSpec((tm, LANES), lambda i: (i, 0))),
        compiler_params=_params(("parallel",)),
        name="rope_tables",
    )(pos_row, inv_freq[:, None], jnp.asarray(spread, BF16), jnp.asarray(sgn))


INPROJ_SUB = 256


def _ret_log_gamma(width, head_w):
    head = _iota((1, width), 1) // head_w
    lg = jnp.zeros((1, width), F32)
    for h in range(RET_HEADS):
        lg = jnp.where(head == h, float(np.log(1.0 - 2.0 ** (-5.0 - h))), lg)
    return lg


def _inproj_kernel(x_ref, g_ref, w_ref, cos_ref, sin_ref, wa2_ref, ba_ref, cw_ref, mb_ref,
                   gq_ref, gk_ref, gla_ref, gv_ref, gr_ref, rq_ref, rks_ref, rkd_ref, rv_ref, rg_ref,
                   mq_ref, mk_ref, mv_ref, mo_ref, mi_ref, mf_ref, cv_ref, *, tiles_per_seq, C):
    tm = x_ref.shape[0]
    sub = INPROJ_SUB
    pos_in_chunk = (_iota((sub, 1), 0) & (C - 1)).astype(F32)
    k_dec = jnp.exp(_ret_log_gamma(QK, RET_DK) * (C - 1.0 - pos_in_chunk))
    lane = _iota((1, SMALL_W), 1)
    gate = (lane >= SM_F) & (lane < SM_F + ML_HEADS)
    low_half = (lane % RET_DK) < RET_DK // 2
    w = cw_ref[...]
    row8 = _iota((8, 2 * MLW), 0)

    @pl.when(pl.program_id(0) % tiles_per_seq == 0)
    def _():
        cv_ref[...] = jnp.zeros_like(cv_ref)

    prev = cv_ref[...]
    for r in range(tm // sub):
        rows = slice(r * sub, (r + 1) * sub)
        hn = _rms(x_ref[rows, :], g_ref[...]).astype(BF16)

        wide = [(a, _dot(hn, w_ref[:, a:b])) for a, b in INPROJ_GROUPS]

        def seg(off, width, wide=wide):
            a, t = [(a, t) for a, t in wide if a <= off][-1]
            return t[:, off - a:off - a + width]

        g1 = seg(OFF_G1, LANES)
        gq_ref[rows, 0:QK0] = seg(OFF_GQ, QK0)
        gq_ref[rows, QK0:QK] = g1[:, :QK1]
        gk_ref[rows, 0:QK0] = seg(OFF_GK, QK0)
        gk_ref[rows, QK0:QK] = pltpu.roll(g1, QK1, 1)[:, :QK1]
        gv_ref[rows, :] = seg(OFF_GV, VW).astype(BF16)
        gr_ref[rows, :] = _silu(seg(OFF_GR, VW)).astype(BF16)
        sm = seg(OFF_SM, SMALL_W)
        z = _dot(sm.astype(BF16), wa2_ref[...]) + ba_ref[...]
        gla_ref[rows, :] = _log_sigmoid(z) * (1.0 / GLA_TAU)

        c = cos_ref[rows, :]
        s = sin_ref[rows, :]

        def rot(t):
            swapped = jnp.where(low_half, pltpu.roll(t, LANES - RET_DK // 2, 1), pltpu.roll(t, RET_DK // 2, 1))
            return t * c + swapped * s

        r1 = rot(seg(OFF_R1, LANES))
        rq_ref[rows, 0:QK0] = rot(seg(OFF_RQ, QK0)).astype(BF16)
        rq_ref[rows, QK0:QK] = r1[:, :QK1].astype(BF16)
        rk0 = rot(seg(OFF_RK, QK0)) * (RET_DK ** -0.5)
        rk1 = pltpu.roll(r1, QK1, 1)[:, :QK1] * (RET_DK ** -0.5)
        rks_ref[rows, 0:QK0] = rk0.astype(BF16)
        rks_ref[rows, QK0:QK] = rk1.astype(BF16)
        rkd_ref[rows, 0:QK0] = (rk0 * k_dec[:, 0:QK0]).astype(BF16)
        rkd_ref[rows, QK0:QK] = (rk1 * k_dec[:, QK0:QK]).astype(BF16)
        rv_ref[rows, :] = seg(OFF_RV, VW).astype(BF16)
        rg_ref[rows, :] = _silu(seg(OFF_RG, VW)).astype(BF16)

        x = seg(OFF_MQK, 2 * MLW)
        x8 = x[0:8, :]
        acc = x * w[ML_CONV - 1:ML_CONV, :]
        acc8 = x8 * w[ML_CONV - 1:ML_CONV, :]
        for sft in range(1, ML_CONV):
            tap = w[ML_CONV - 1 - sft:ML_CONV - sft, :]
            acc = acc + pltpu.roll(x, sft, 0) * tap
            acc8 = acc8 + jnp.where(row8 < sft, pltpu.roll(prev, sft, 0), pltpu.roll(x8, sft, 0)) * tap
        prev = x[sub - 8:sub, :]
        y = _silu(jnp.concatenate([acc8, acc[8:, :]], axis=0))
        mq_ref[rows, :] = y[:, :MLW].astype(BF16)
        mk_ref[rows, :] = (y[:, MLW:] * (ML_DK ** -0.5)).astype(BF16)
        mv_ref[rows, :] = seg(OFF_MV, MLW).astype(BF16)
        mo_ref[rows, :] = _sigmoid(seg(OFF_MO, MLW)).astype(BF16)
        pre = sm + mb_ref[...]
        mi_ref[rows, :] = jnp.where(gate, pltpu.roll(pre, SM_F - SM_I, 1), 0.0)
        mf_ref[rows, :] = jnp.where(gate, _log_sigmoid(pre), 0.0)
    cv_ref[...] = prev


def _inproj(x2d, l, g, w_cat, cos_t, sin_t, wa2p, ba, cw, mb, *, S, tm=TOKEN_TILE, C=MIX_CHUNK):
    n, d = x2d.shape
    widths = [(QK, F32), (QK, F32), (QK, F32), (VW, BF16), (VW, BF16),
              (QK, BF16), (QK, BF16), (QK, BF16), (VW, BF16), (VW, BF16),
              (MLW, BF16), (MLW, BF16), (MLW, BF16), (MLW, BF16), (SMALL_W, F32), (SMALL_W, F32)]
    row = lambda w: pl.BlockSpec((tm, w), lambda i: (i, 0))
    return pl.pallas_call(
        functools.partial(_inproj_kernel, tiles_per_seq=S // tm, C=C),
        out_shape=tuple(jax.ShapeDtypeStruct((n, w), dt) for w, dt in widths),
        grid=(n // tm,),
        in_specs=[row(d), _layer(l, (1, d), 1), _layer(l, (d, W_CAT), 1), row(LANES), row(LANES),
                  _layer(l, (SMALL_W, QK), 1), _layer(l, (1, QK), 1), _layer(l, (ML_CONV, 2 * MLW), 1),
                  _layer(l, (1, SMALL_W), 1)],
        out_specs=tuple(row(w) for w, _ in widths),
        scratch_shapes=[pltpu.VMEM((8, 2 * MLW), F32)],
        compiler_params=_params(("arbitrary",)),
        name="inproj",
    )(x2d, g, w_cat, cos_t, sin_t, wa2p, ba, cw, mb)


def _build_w_cat(w_in):
    offs = np.concatenate([[0], np.cumsum(IN_SIZES)])
    w_in = w_in.astype(BF16)
    col = lambda i: w_in[:, :, offs[i]:offs[i + 1]]
    (gq, gk, gv, ga, gr, rq, rk, rv, rg, mq, mk, mv, mo, mi, mf) = [col(i) for i in range(15)]
    small = jnp.concatenate([ga, mi, mf], axis=-1)
    small = jnp.pad(small, ((0, 0), (0, 0), (0, SMALL_W - small.shape[-1])))
    pieces = [gq[..., :QK0], gk[..., :QK0], gq[..., QK0:], gk[..., QK0:], rq[..., QK0:], rk[..., QK0:],
              rq[..., :QK0], rk[..., :QK0], gv, gr, rv, rg, mq, mk, mv, mo, small]
    w_cat = jnp.concatenate(pieces, axis=-1).astype(BF16)
    assert w_cat.shape[-1] == W_CAT
    return w_cat


def _pair_sum(x, lo):
    s_lo = jnp.sum(jnp.where(lo, x, 0.0), axis=-1, keepdims=True)
    s_hi = jnp.sum(jnp.where(lo, 0.0, x), axis=-1, keepdims=True)
    return jnp.where(lo, s_lo, s_hi)


def _pair_dot(s_a, s_b, v_a, v_b):
    return _dot(jnp.concatenate([s_a, s_b], axis=1), jnp.concatenate([v_a, v_b], axis=0))


def _qk_tiles(q_bf, k_bf, head_w):
    width = q_bf.shape[-1]
    out = []
    for h in range(width // head_w):
        t0 = (h * head_w) // LANES * LANES
        t1 = min(t0 + LANES, width)
        m = _lane_mask(t1 - t0, h * head_w - t0, (h + 1) * head_w - t0)
        out.append((q_bf[:, t0:t1] * m, k_bf[:, t0:t1]))
    return out


def _gla_kernel(q_ref, k_ref, la_ref, v_ref, r_ref, gn_ref, o_ref, st_ref, oacc_ref, *, T, C):
    @pl.when(pl.program_id(1) == 0)
    def _():
        st_ref[...] = jnp.zeros_like(st_ref)

    n_chunks = T // C
    causal = _iota((C, C), 0) >= _iota((C, C), 1)
    tri = causal.astype(BF16)
    bd = (_iota((VW, QK), 0) // GLA_DV) == (_iota((VW, QK), 1) // GLA_DK)
    lo_bf = _lane_mask(LANES, 0, GLA_DV)
    hi_bf = _lane_mask(LANES, GLA_DV, LANES)
    lo = _iota((1, LANES), 1) < GLA_DV
    scale = GLA_DK ** -0.5

    bl_min = jnp.sum(la_ref[0:C, :], axis=0, keepdims=True)
    for c in range(1, n_chunks):
        bl_min = jnp.minimum(bl_min, jnp.sum(la_ref[c * C:(c + 1) * C, :], axis=0, keepdims=True))
    safe = jnp.min(bl_min) > -GLA_SAFE_LOG

    def finish(o, rws, cols):
        ms = _pair_sum(o * o, lo) * (1.0 / GLA_DV)
        return (o * lax.rsqrt(ms + EPS) * gn_ref[:, cols] * r_ref[rws, cols].astype(F32)).astype(BF16)

    @pl.when(safe)
    def _():
        bcs = [_cumsum_rows(tri, la_ref[c * C:(c + 1) * C, :]) for c in range(n_chunks)]
        def front(c):
            rows = slice(c * C, (c + 1) * C)
            bc = bcs[c]
            bl = bc[C - 1:C, :]
            k = k_ref[rows, :]
            qt = (q_ref[rows, :] * (scale * jnp.exp(bc))).astype(BF16)
            kt = (k * jnp.exp(-bc)).astype(BF16)
            kh = (k * jnp.exp(bl - bc)).astype(BF16)
            raw = [_dot_nt(qm, kk) for qm, kk in _qk_tiles(qt, kt, GLA_DK)]
            return rows, bl, qt, raw, _dot_tn(v_ref[rows, :], kh)

        st = st_ref[...]
        nxt = front(0)
        for c in range(n_chunks):
            rows, bl, qt, raw, upd = nxt
            if c + 1 < n_chunks:
                nxt = front(c + 1)
            sc = [s.astype(BF16) * tri for s in raw]
            inter = _dot_nt(qt, st.astype(BF16))
            for p in range(GLA_HEADS // 2):
                cols = slice(LANES * p, LANES * (p + 1))
                vp = v_ref[rows, cols]
                o = _pair_dot(sc[2 * p], sc[2 * p + 1], vp * lo_bf, vp * hi_bf) + inter[:, cols]
                o_ref[rows, cols] = finish(o, rows, cols)
            st = st * jnp.exp(bl) + jnp.where(bd, upd, 0.0)
        st_ref[...] = st

    @pl.when(jnp.logical_not(safe))
    def _():
        vb = v_ref[...]
        row_id = _iota((T, 2 * LANES), 0)

        def token(t, carry):
            a_t = jnp.exp(la_ref[pl.ds(t, 1), :])
            k_t = k_ref[pl.ds(t, 1), :]
            q_t = q_ref[pl.ds(t, 1), :] * scale
            v_col = _dot_tn(vb, (row_id == t).astype(BF16))[:, :QK]
            st = st_ref[...] * a_t + jnp.where(bd, v_col * k_t, 0.0)
            st_ref[...] = st
            q8 = jnp.broadcast_to(q_t, (8, QK)).astype(BF16)
            oacc_ref[pl.ds(t, 1), :] = _dot_nt(q8, st.astype(BF16))[0:1, :]
            return carry

        lax.fori_loop(0, T, token, 0)
        for p in range(GLA_HEADS // 2):
            cols = slice(LANES * p, LANES * (p + 1))
            o_ref[:, cols] = finish(oacc_ref[:, cols], slice(0, T), cols)


def _mix_specs(S, T):
    nt = S // T
    row = lambda w: pl.BlockSpec((T, w), lambda b, i: (b * nt + i, 0))
    full = lambda a, c: pl.BlockSpec((a, c), lambda b, i: (0, 0))
    return nt, row, full


def _gla(q, k, la, v, r, l, gn, *, B, S, T=MIX_TILE_LONG, C=MIX_CHUNK):
    nt, row, full = _mix_specs(S, T)
    return pl.pallas_call(
        functools.partial(_gla_kernel, T=T, C=C),
        out_shape=jax.ShapeDtypeStruct((B * S, VW), BF16),
        grid=(B, nt),
        in_specs=[row(QK), row(QK), row(QK), row(VW), row(VW), _layer(l, (1, VW), 2)],
        out_specs=row(VW),
        scratch_shapes=[pltpu.VMEM((VW, QK), F32), pltpu.VMEM((T, VW), F32)],
        compiler_params=_params(("arbitrary", "arbitrary")),
        name="gla",
    )(q, k, la, v, r, gn)


def _ret_kernel(q_ref, ks_ref, kd_ref, v_ref, rg_ref, gn_ref, o_ref, rt_ref, dec_ref, *, T, C):
    @pl.when(pl.program_id(1) == 0)
    def _():
        rt_ref[...] = jnp.zeros_like(rt_ref)
        rel = (_iota((C, C), 0) - _iota((C, C), 1)).astype(F32)
        for h in range(RET_HEADS):
            log_g = float(np.log(1.0 - 2.0 ** (-5.0 - h)))
            dec_ref[h] = jnp.where(rel >= 0.0, jnp.exp(log_g * jnp.maximum(rel, 0.0)), 0.0)

    lo_bf = _lane_mask(LANES, 0, RET_DV)
    hi_bf = _lane_mask(LANES, RET_DV, LANES)
    lo = _iota((1, LANES), 1) < RET_DV
    idx = _iota((C, 1), 0).astype(F32)
    q_dec = jnp.exp(_ret_log_gamma(VW, RET_DV) * (idx + 1.0))
    chunk_dec = jnp.exp(_ret_log_gamma(QK, RET_DK) * float(C))
    bd = (_iota((VW, QK), 0) // RET_DV) == (_iota((VW, QK), 1) // RET_DK)

    def front(c):
        rows = slice(c * C, (c + 1) * C)
        q = q_ref[rows, :]
        raw = [_dot_nt(qm, kk) for qm, kk in _qk_tiles(q, ks_ref[rows, :], RET_DK)]
        return rows, q, raw, _dot_tn(v_ref[rows, :], kd_ref[rows, :])

    rt = rt_ref[...]
    n_chunks = T // C
    nxt = front(0)
    for c in range(n_chunks):
        rows, q, raw, upd = nxt
        if c + 1 < n_chunks:
            nxt = front(c + 1)
        sc = [(s * dec_ref[h]).astype(BF16) for h, s in enumerate(raw)]
        inter = _dot_nt(q, rt.astype(BF16)) * q_dec
        for p in range(RET_HEADS // 2):
            cols = slice(LANES * p, LANES * (p + 1))
            vp = v_ref[rows, cols]
            o = _pair_dot(sc[2 * p], sc[2 * p + 1], vp * lo_bf, vp * hi_bf) + inter[:, cols]
            mu = _pair_sum(o, lo) * (1.0 / RET_DV)
            xc = o - mu
            var = _pair_sum(xc * xc, lo) * (1.0 / RET_DV)
            y = xc * lax.rsqrt(var + EPS) * gn_ref[:, cols] * rg_ref[rows, cols].astype(F32)
            o_ref[rows, cols] = y.astype(BF16)
        rt = rt * chunk_dec + jnp.where(bd, upd, 0.0)
    rt_ref[...] = rt


def _ret(q, ks, kd, v, rg, l, gn, *, B, S, T=MIX_TILE, C=MIX_CHUNK):
    nt, row, full = _mix_specs(S, T)
    return pl.pallas_call(
        functools.partial(_ret_kernel, T=T, C=C),
        out_shape=jax.ShapeDtypeStruct((B * S, VW), BF16),
        grid=(B, nt),
        in_specs=[row(QK), row(QK), row(QK), row(VW), row(VW), _layer(l, (1, VW), 2)],
        out_specs=row(VW),
        scratch_shapes=[pltpu.VMEM((VW, QK), F32), pltpu.VMEM((RET_HEADS, C, C), F32)],
        compiler_params=_params(("arbitrary", "arbitrary")),
        name="retention",
    )(q, ks, kd, v, rg, gn)


def _mlstm_kernel(q_ref, k_ref, v_ref, og_ref, mi_ref, mf_ref, o_ref, ct_ref, n_ref, m_ref, *, T, C):
    @pl.when(pl.program_id(1) == 0)
    def _():
        ct_ref[...] = jnp.zeros_like(ct_ref)
        n_ref[...] = jnp.zeros_like(n_ref)
        m_ref[...] = jnp.zeros_like(m_ref)

    causal = _iota((C, C), 0) >= _iota((C, C), 1)
    tri = causal.astype(BF16)
    bd = (_iota((MLW, MLW), 0) // ML_DV) == (_iota((MLW, MLW), 1) // ML_DK)
    bd_bf = bd.astype(BF16)
    lo_bf = _lane_mask(LANES, 0, ML_DV)
    hi_bf = _lane_mask(LANES, ML_DV, LANES)
    lo = _iota((1, LANES), 1) < ML_DV
    widen = (_iota((LANES, MLW), 0) == (_iota((LANES, MLW), 1) // ML_DV + SM_F)).astype(BF16)
    ones_blk = jnp.ones((C, LANES), BF16)
    row_id = _iota((C, LANES), 0)

    n_chunks = T // C
    rows = [slice(c * C, (c + 1) * C) for c in range(n_chunks)]
    i_pre = [mi_ref[r, :] for r in rows]
    f_cum = [_cumsum_rows(tri, mf_ref[r, :]) for r in rows]
    qs = [q_ref[r, :] for r in rows]
    ks = [k_ref[r, :] for r in rows]
    raw = [[_dot_nt(qm, kk) for qm, kk in _qk_tiles(q, k, ML_DK)] for q, k in zip(qs, ks)]

    g_t, g_max, b_last, log_w, w_max = [], [], [], [], []
    for c in range(n_chunks):
        g = i_pre[c] - f_cum[c]
        g_t.append(g.T)
        run = g
        sft = 1
        while sft < C:
            run = jnp.maximum(run, jnp.where(row_id >= sft, pltpu.roll(run, sft, 0), -jnp.inf))
            sft *= 2
        g_max.append(run)
        b_last.append(f_cum[c][C - 1:C, :])
        log_w.append(b_last[c] - f_cum[c] + i_pre[c])
        w_max.append(jnp.max(log_w[c], axis=0, keepdims=True))
    ms = [m_ref[...]]
    for c in range(n_chunks):
        ms.append(jnp.maximum(b_last[c] + ms[c], w_max[c]))

    mm, wide = [], []
    for c in range(n_chunks):
        mm.append(jnp.maximum(g_max[c], ms[c]))
        slab = jnp.concatenate([jnp.exp(ms[c] - mm[c]), jnp.exp(-(f_cum[c] + mm[c])), jnp.exp(log_w[c] - ms[c + 1]),
                                jnp.broadcast_to(jnp.exp(b_last[c] + ms[c] - ms[c + 1]), (8, LANES))], axis=0)
        s_hi, s_lo = _split2(slab)
        wide.append(_dot(s_hi, widen) + _dot(s_lo, widen))
    w_inter = [w[0:C] for w in wide]
    e_negm = [w[C:2 * C] for w in wide]
    gf = [w[3 * C:3 * C + 1] for w in wide]
    wk = [w[2 * C:3 * C] * k.astype(F32) for w, k in zip(wide, ks)]
    upd = [_dot_tn(v_ref[r, :], x.astype(BF16)) for r, x in zip(rows, wk)]

    cts, ns = [ct_ref[...]], [n_ref[...]]
    for c in range(n_chunks):
        cts.append(cts[c] * gf[c] + jnp.where(bd, upd[c], 0.0))
        ns.append(ns[c] * gf[c] + jnp.sum(wk[c], axis=0, keepdims=True))
    inter = [_dot_nt(qs[c], jnp.concatenate(
        [cts[c].astype(BF16), jnp.broadcast_to(ns[c].astype(BF16), (MLW, MLW)) * bd_bf], axis=0))
        for c in range(n_chunks)]

    res = []
    for c in range(n_chunks):
        per_head = []
        for h in range(ML_HEADS):
            lane = SM_F + h
            dm = jnp.where(causal, jnp.exp(g_t[c][lane:lane + 1, :] - mm[c][:, lane:lane + 1]), 0.0)
            s = (raw[c][h] * dm).astype(BF16)
            vp = v_ref[rows[c], LANES * (h // 2):LANES * (h // 2 + 1)]
            v_ext = jnp.concatenate([vp * (lo_bf if h % 2 == 0 else hi_bf), ones_blk], axis=1)
            per_head.append(_dot(s, v_ext))
        res.append(per_head)
    for c in range(n_chunks):
        for p in range(ML_HEADS // 2):
            cols = slice(LANES * p, LANES * (p + 1))
            ra, rb = res[c][2 * p], res[c][2 * p + 1]
            num = ra[:, :LANES] + rb[:, :LANES] + w_inter[c][:, cols] * inter[c][:, cols]
            den = (jnp.where(lo, ra[:, LANES:], rb[:, LANES:])
                   + w_inter[c][:, cols] * inter[c][:, MLW + LANES * p:MLW + LANES * (p + 1)])
            h_t = num / jnp.maximum(jnp.abs(den), e_negm[c][:, cols])
            o_ref[rows[c], cols] = (og_ref[rows[c], cols].astype(F32) * h_t).astype(BF16)
    ct_ref[...] = cts[n_chunks]
    n_ref[...] = ns[n_chunks]
    m_ref[...] = ms[n_chunks]


def _mlstm(q, k, v, og, mi, mf, *, B, S, T=MIX_TILE_LONG, C=MIX_CHUNK):
    nt, row, full = _mix_specs(S, T)
    return pl.pallas_call(
        functools.partial(_mlstm_kernel, T=T, C=C),
        out_shape=jax.ShapeDtypeStruct((B * S, MLW), BF16),
        grid=(B, nt),
        in_specs=[row(MLW), row(MLW), row(MLW), row(MLW), row(SMALL_W), row(SMALL_W)],
        out_specs=row(MLW),
        scratch_shapes=[pltpu.VMEM((MLW, MLW), F32), pltpu.VMEM((1, MLW), F32), pltpu.VMEM((1, LANES), F32)],
        compiler_params=_params(("arbitrary", "arbitrary")),
        name="mlstm",
    )(q, k, v, og, mi, mf)


def _memkv_kernel(mem_ref, g_ref, w_ref, k_ref, v_ref):
    mn = _rms(mem_ref[0], g_ref[...]).astype(BF16)
    d = mem_ref.shape[-1]
    k_ref[0] = _dot(mn, w_ref[:, :d]).astype(BF16)
    v_ref[0] = _dot(mn, w_ref[:, d:]).astype(BF16)


def _memkv(mem, l, g, w_kv):
    b, m, d = mem.shape
    blk = pl.BlockSpec((1, m, d), lambda i: (i, 0, 0))
    return pl.pallas_call(
        _memkv_kernel,
        out_shape=(jax.ShapeDtypeStruct((b, m, d), BF16), jax.ShapeDtypeStruct((b, m, d), BF16)),
        grid=(b,),
        in_specs=[blk, _layer(l, (1, d), 1), _layer(l, (d, 2 * d), 1)],
        out_specs=(blk, blk),
        compiler_params=_params(("parallel",)),
        name="mem_kv",
    )(mem, g, w_kv)


OUTXA_SUB = 256


def _outxa_kernel(x_ref, og_ref, or_ref, om_ref, wo_ref, g_ref, wq_ref, k_ref, v_ref, wxo_ref, o_ref):
    subs = [slice(r * OUTXA_SUB, (r + 1) * OUTXA_SUB) for r in range(x_ref.shape[0] // OUTXA_SUB)]
    x1 = [x_ref[r, :] + _dot(jnp.concatenate([og_ref[r, :], or_ref[r, :], om_ref[r, :]], axis=-1), wo_ref[...])
          for r in subs]
    hn = [_rms(t, g_ref[...]).astype(BF16) for t in x1]
    q = [(_dot(t, wq_ref[...]) * (XA_DH ** -0.5)).astype(BF16) for t in hn]
    heads = [slice(XA_DH * h, XA_DH * (h + 1)) for h in range(XA_HEADS)]
    s = [[_dot_nt(t[:, c], k_ref[0, :, c]) for c in heads] for t in q]
    p = []
    for per_head in s:
        e = [jnp.exp(t - jnp.max(t, axis=-1, keepdims=True)) for t in per_head]
        p.append([(t / jnp.sum(t, axis=-1, keepdims=True)).astype(BF16) for t in e])
    o = [jnp.concatenate([_dot(t, v_ref[0, :, c]).astype(BF16) for t, c in zip(per_head, heads)], axis=-1)
         for per_head in p]
    for r, t, u in zip(subs, x1, o):
        o_ref[r, :] = t + _dot(u, wxo_ref[...])


def _outxa(x2d, o_gla, o_ret, o_ml, l, w_out, g_xa, w_q, mem_k, mem_v, w_o, *, B, S, tm=TOKEN_TILE):
    n, d = x2d.shape
    nt = S // tm
    m = mem_k.shape[1]
    row = lambda w: pl.BlockSpec((tm, w), lambda b, i: (b * nt + i, 0))
    kv = pl.BlockSpec((1, m, d), lambda b, i: (b, 0, 0))
    return pl.pallas_call(
        _outxa_kernel,
        out_shape=jax.ShapeDtypeStruct((n, d), F32),
        grid=(B, nt),
        in_specs=[row(d), row(VW), row(VW), row(MLW), _layer(l, (d, d), 2), _layer(l, (1, d), 2),
                  _layer(l, (d, d), 2), kv, kv, _layer(l, (d, d), 2)],
        out_specs=row(d),
        compiler_params=_params(("parallel", "parallel")),
        name="outproj_xattn",
    )(x2d, o_gla, o_ret, o_ml, w_out, g_xa, w_q, mem_k, mem_v, w_o)


def kernel(x, mem, positions, g_ffa, w_ffa_gu, w_ffa_down, g_mix, w_in, gla_w_a2, gla_b_a, gla_g_norm,
           ret_g_norm, ml_conv, ml_b_i, ml_b_f, w_out, g_xa, g_mem, w_xa_q, w_xa_kv, w_xa_o, g_ffb,
           w_ffb_gu, w_ffb_down, g_final):
    B, S, D = x.shape
    L = w_in.shape[0]
    N = B * S
    bf = lambda t: t.astype(BF16)
    vec = lambda t: t[:, None, :]
    w_ffa_gu, w_ffa_down, w_ffb_gu, w_ffb_down = bf(w_ffa_gu), bf(w_ffa_down), bf(w_ffb_gu), bf(w_ffb_down)
    w_out_b, w_q_b, w_kv_b, w_o_b = bf(w_out), bf(w_xa_q), bf(w_xa_kv), bf(w_xa_o)
    w_cat = _build_w_cat(w_in)
    wa2p = bf(jnp.pad(gla_w_a2, ((0, 0), (0, SMALL_W - GLA_RANK), (0, 0))))
    ml_bias = jnp.pad(jnp.concatenate([ml_b_i, ml_b_f], axis=-1),
                      ((0, 0), (SM_I, SMALL_W - SM_I - 2 * ML_HEADS)))
    g_ffa, g_mix, g_xa, g_mem, g_ffb = vec(g_ffa), vec(g_mix), vec(g_xa), vec(g_mem), vec(g_ffb)
    gla_b_a, gla_g_norm, ret_g_norm, ml_bias = vec(gla_b_a), vec(gla_g_norm), vec(ret_g_norm), vec(ml_bias)
    g_final = g_final[None, :]

    cos_t, sin_t = _rope_tables(positions.reshape(1, N))
    h = x.reshape(N, D)
    for l in range(L):
        h = _ffn(h, l, g_ffa, w_ffa_gu, w_ffa_down, g_final, final=False)
        (gq, gk, gla, gv, gr, rq, rks, rkd, rv, rg, mq, mk, mv, mo, mi, mf) = _inproj(
            h, l, g_mix, w_cat, cos_t, sin_t, wa2p, gla_b_a, ml_conv, ml_bias, S=S)
        o_gla = _gla(gq, gk, gla, gv, gr, l, gla_g_norm, B=B, S=S)
        o_ret = _ret(rq, rks, rkd, rv, rg, l, ret_g_norm, B=B, S=S)
        o_ml = _mlstm(mq, mk, mv, mo, mi, mf, B=B, S=S)
        mem_k, mem_v = _memkv(mem, l, g_mem, w_kv_b)
        h = _outxa(h, o_gla, o_ret, o_ml, l, w_out_b, g_xa, w_q_b, mem_k, mem_v, w_o_b, B=B, S=S)
        h = _ffn(h, l, g_ffb, w_ffb_gu, w_ffb_down, g_final, final=(l == L - 1))
    return h.reshape(B, S, D)
```

```python
import functools

import numpy as np
import jax
import jax.numpy as jnp
from jax import lax
from jax.experimental import pallas as pl
from jax.experimental.pallas import tpu as pltpu

F32 = jnp.float32
BF16 = jnp.bfloat16
EPS = 1e-6

D_MODEL = 1024
GLA_HEADS, GLA_DK, GLA_DV, GLA_RANK, GLA_TAU = 6, 32, 64, 16, 16.0
RET_HEADS, RET_DK, RET_DV = 6, 32, 64
ML_HEADS, ML_DK, ML_DV, ML_CONV = 4, 64, 64, 4
ROPE_BASE = 10000.0
XA_HEADS = 4
XA_DH = D_MODEL // XA_HEADS
IN_SIZES = (
    GLA_HEADS * GLA_DK, GLA_HEADS * GLA_DK, GLA_HEADS * GLA_DV, GLA_RANK, GLA_HEADS * GLA_DV,
    RET_HEADS * RET_DK, RET_HEADS * RET_DK, RET_HEADS * RET_DV, RET_HEADS * RET_DV,
    ML_HEADS * ML_DK, ML_HEADS * ML_DK, ML_HEADS * ML_DV, ML_HEADS * ML_DV, ML_HEADS, ML_HEADS,
)

LANES = 128
QK = GLA_HEADS * GLA_DK
VW = GLA_HEADS * GLA_DV
MLW = ML_HEADS * ML_DK
SMALL_W = LANES
SM_I = GLA_RANK
SM_F = GLA_RANK + ML_HEADS

QK0 = 4 * GLA_DK
QK1 = QK - QK0
OFF_GQ, OFF_GK, OFF_G1, OFF_R1, OFF_RQ, OFF_RK = 0, 128, 256, 384, 512, 640
OFF_GV, OFF_GR, OFF_RV, OFF_RG = 768, 1152, 1536, 1920
OFF_MQK, OFF_MV, OFF_MO, OFF_SM = 2304, 2816, 3072, 3328
W_CAT = 3456
INPROJ_GROUPS = ((OFF_GQ, OFF_GV), (OFF_GV, OFF_RV), (OFF_RV, OFF_MQK), (OFF_MQK, OFF_MV), (OFF_MV, W_CAT))

MIX_CHUNK = 128
MIX_TILE = 1024
MIX_TILE_LONG = 2048
TOKEN_TILE = 1024
GLA_SAFE_LOG = 40.0
VMEM_LIMIT = 48 * 1024 * 1024


def _dot(a, b):
    return jnp.dot(a, b, preferred_element_type=F32)


def _dot_nt(a, b):
    return lax.dot_general(a, b, (((1,), (1,)), ((), ())), preferred_element_type=F32)


def _dot_tn(a, b):
    return lax.dot_general(a, b, (((0,), (0,)), ((), ())), preferred_element_type=F32)


def _sigmoid(x):
    return 1.0 / (1.0 + jnp.exp(-x))


def _silu(x):
    return x * _sigmoid(x)


def _log_sigmoid(x):
    return jnp.minimum(x, 0.0) - jnp.log1p(jnp.exp(-jnp.abs(x)))


def _rms(x, g):
    return x * lax.rsqrt(jnp.mean(x * x, axis=-1, keepdims=True) + EPS) * g


def _split2(x):
    hi = x.astype(BF16)
    return hi, (x - hi.astype(F32)).astype(BF16)


def _split3(x):
    hi = x.astype(BF16)
    r1 = x - hi.astype(F32)
    mid = r1.astype(BF16)
    return hi, mid, (r1 - mid.astype(F32)).astype(BF16)


def _cumsum_rows(tri_bf, x):
    hi, mid, lo = _split3(x)
    return _dot(tri_bf, hi) + _dot(tri_bf, mid) + _dot(tri_bf, lo)


def _iota(shape, dim):
    return lax.broadcasted_iota(jnp.int32, shape, dim)


def _lane_mask(width, lo, hi, dtype=BF16):
    lane = _iota((1, width), 1)
    return ((lane >= lo) & (lane < hi)).astype(dtype)


def _params(sem):
    return pltpu.CompilerParams(dimension_semantics=sem, vmem_limit_bytes=VMEM_LIMIT)


def _layer(l, tail, grid_rank):
    idx = (l,) + (0,) * len(tail)
    imap = (lambda i: idx) if grid_rank == 1 else (lambda b, i: idx)
    return pl.BlockSpec((None,) + tuple(tail), imap, pipeline_mode=pl.Buffered(1))


def _const(shape, grid_rank):
    idx = (0,) * len(shape)
    imap = (lambda i: idx) if grid_rank == 1 else (lambda b, i: idx)
    return pl.BlockSpec(tuple(shape), imap, pipeline_mode=pl.Buffered(1))


FFN_SUB = 256
FFN_CHUNK = 1024


def _ffn_kernel(x_ref, g_ref, wgu_ref, wd_ref, gf_ref, o_ref, *, final):
    tm = x_ref.shape[0]
    d_ff = wd_ref.shape[0]
    for r in range(tm // FFN_SUB):
        rows = slice(r * FFN_SUB, (r + 1) * FFN_SUB)
        x = x_ref[rows, :]
        hn = _rms(x, g_ref[...]).astype(BF16)
        acc = None
        for off in range(0, d_ff, FFN_CHUNK):
            cw = min(FFN_CHUNK, d_ff - off)
            a = _dot(hn, wgu_ref[:, off:off + cw])
            g = _dot(hn, wgu_ref[:, d_ff + off:d_ff + off + cw])
            part = _dot((_silu(a) * g).astype(BF16), wd_ref[off:off + cw, :])
            acc = part if acc is None else acc + part
        y = x + 0.5 * acc
        if final:
            y = _rms(y, gf_ref[...])
        o_ref[rows, :] = y


def _ffn(x2d, l, g, w_gu, w_down, g_final, *, final, tm=TOKEN_TILE):
    n, d = x2d.shape
    d_ff = w_down.shape[1]
    return pl.pallas_call(
        functools.partial(_ffn_kernel, final=final),
        out_shape=jax.ShapeDtypeStruct((n, d), F32),
        grid=(n // tm,),
        in_specs=[
            pl.BlockSpec((tm, d), lambda i: (i, 0)),
            _layer(l, (1, d), 1),
            _layer(l, (d, 2 * d_ff), 1),
            _layer(l, (d_ff, d), 1),
            _const((1, d), 1),
        ],
        out_specs=pl.BlockSpec((tm, d), lambda i: (i, 0)),
        compiler_params=_params(("parallel",)),
        name="ffn",
    )(x2d, g, w_gu, w_down, g_final)


def _rope_kernel(pos_ref, invf_ref, spread_ref, sgn_ref, cos_ref, sin_ref):
    ang = invf_ref[...] * pos_ref[...].astype(F32)
    spread = spread_ref[...]

    def to_rows(t):
        return sum(_dot_tn(part, spread) for part in _split3(t))

    cos_ref[...] = to_rows(jnp.cos(ang))
    sin_ref[...] = to_rows(jnp.sin(ang)) * sgn_ref[...]


def _rope_tables(pos_row, *, tm=2048):
    n = pos_row.shape[1]
    tm = min(tm, n)
    half = RET_DK // 2
    inv_freq = 1.0 / (ROPE_BASE ** jnp.linspace(0.0, 1.0, half, dtype=F32))
    spread = (np.arange(half)[:, None] == (np.arange(LANES)[None, :] % half)).astype(np.float32)
    sgn = np.where(np.arange(LANES) % RET_DK < half, -1.0, 1.0).astype(np.float32)[None, :]
    return pl.pallas_call(
        _rope_kernel,
        out_shape=(jax.ShapeDtypeStruct((n, LANES), F32), jax.ShapeDtypeStruct((n, LANES), F32)),
        grid=(n // tm,),
        in_specs=[
            pl.BlockSpec((1, tm), lambda i: (0, i)),
            _const((half, 1), 1),
            _const((half, LANES), 1),
            _const((1, LANES), 1),
        ],
        out_specs=(pl.BlockSpec((tm, LANES), lambda i: (i, 0)), pl.BlockSpec((tm, LANES), lambda i: (i, 0))),
        compiler_params=_params(("parallel",)),
        name="rope_tables",
    )(pos_row, inv_freq[:, None], jnp.asarray(spread, BF16), jnp.asarray(sgn))


INPROJ_SUB = 256


def _ret_log_gamma(width, head_w):
    head = _iota((1, width), 1) // head_w
    lg = jnp.zeros((1, width), F32)
    for h in range(RET_HEADS):
        lg = jnp.where(head == h, float(np.log(1.0 - 2.0 ** (-5.0 - h))), lg)
    return lg


def _inproj_kernel(x_ref, g_ref, w_ref, cos_ref, sin_ref, wa2_ref, ba_ref, cw_ref, mb_ref,
                   gq_ref, gk_ref, gla_ref, gv_ref, gr_ref, rq_ref, rks_ref, rkd_ref, rv_ref, rg_ref,
                   mq_ref, mk_ref, mv_ref, mo_ref, mi_ref, mf_ref, cv_ref, *, tiles_per_seq, C):
    tm = x_ref.shape[0]
    sub = INPROJ_SUB
    pos_in_chunk = (_iota((sub, 1), 0) & (C - 1)).astype(F32)
    k_dec = jnp.exp(_ret_log_gamma(QK, RET_DK) * (C - 1.0 - pos_in_chunk))
    lane = _iota((1, SMALL_W), 1)
    gate = (lane >= SM_F) & (lane < SM_F + ML_HEADS)
    low_half = (lane % RET_DK) < RET_DK // 2
    w = cw_ref[...]
    row8 = _iota((8, 2 * MLW), 0)

    @pl.when(pl.program_id(0) % tiles_per_seq == 0)
    def _():
        cv_ref[...] = jnp.zeros_like(cv_ref)

    prev = cv_ref[...]
    for r in range(tm // sub):
        rows = slice(r * sub, (r + 1) * sub)
        hn = _rms(x_ref[rows, :], g_ref[...]).astype(BF16)

        wide = [(a, _dot(hn, w_ref[:, a:b])) for a, b in INPROJ_GROUPS]

        def seg(off, width, wide=wide):
            a, t = [(a, t) for a, t in wide if a <= off][-1]
            return t[:, off - a:off - a + width]

        g1 = seg(OFF_G1, LANES)
        gq_ref[rows, 0:QK0] = seg(OFF_GQ, QK0)
        gq_ref[rows, QK0:QK] = g1[:, :QK1]
        gk_ref[rows, 0:QK0] = seg(OFF_GK, QK0)
        gk_ref[rows, QK0:QK] = pltpu.roll(g1, QK1, 1)[:, :QK1]
        gv_ref[rows, :] = seg(OFF_GV, VW).astype(BF16)
        gr_ref[rows, :] = _silu(seg(OFF_GR, VW)).astype(BF16)
        sm = seg(OFF_SM, SMALL_W)
        z = _dot(sm.astype(BF16), wa2_ref[...]) + ba_ref[...]
        gla_ref[rows, :] = _log_sigmoid(z) * (1.0 / GLA_TAU)

        c = cos_ref[rows, :]
        s = sin_ref[rows, :]

        def rot(t):
            swapped = jnp.where(low_half, pltpu.roll(t, LANES - RET_DK // 2, 1), pltpu.roll(t, RET_DK // 2, 1))
            return t * c + swapped * s

        r1 = rot(seg(OFF_R1, LANES))
        rq_ref[rows, 0:QK0] = rot(seg(OFF_RQ, QK0)).astype(BF16)
        rq_ref[rows, QK0:QK] = r1[:, :QK1].astype(BF16)
        rk0 = rot(seg(OFF_RK, QK0)) * (RET_DK ** -0.5)
        rk1 = pltpu.roll(r1, QK1, 1)[:, :QK1] * (RET_DK ** -0.5)
        rks_ref[rows, 0:QK0] = rk0.astype(BF16)
        rks_ref[rows, QK0:QK] = rk1.astype(BF16)
        rkd_ref[rows, 0:QK0] = (rk0 * k_dec[:, 0:QK0]).astype(BF16)
        rkd_ref[rows, QK0:QK] = (rk1 * k_dec[:, QK0:QK]).astype(BF16)
        rv_ref[rows, :] = seg(OFF_RV, VW).astype(BF16)
        rg_ref[rows, :] = _silu(seg(OFF_RG, VW)).astype(BF16)

        x = seg(OFF_MQK, 2 * MLW)
        x8 = x[0:8, :]
        acc = x * w[ML_CONV - 1:ML_CONV, :]
        acc8 = x8 * w[ML_CONV - 1:ML_CONV, :]
        for sft in range(1, ML_CONV):
            tap = w[ML_CONV - 1 - sft:ML_CONV - sft, :]
            acc = acc + pltpu.roll(x, sft, 0) * tap
            acc8 = acc8 + jnp.where(row8 < sft, pltpu.roll(prev, sft, 0), pltpu.roll(x8, sft, 0)) * tap
        prev = x[sub - 8:sub, :]
        y = _silu(jnp.concatenate([acc8, acc[8:, :]], axis=0))
        mq_ref[rows, :] = y[:, :MLW].astype(BF16)
        mk_ref[rows, :] = (y[:, MLW:] * (ML_DK ** -0.5)).astype(BF16)
        mv_ref[rows, :] = seg(OFF_MV, MLW).astype(BF16)
        mo_ref[rows, :] = _sigmoid(seg(OFF_MO, MLW)).astype(BF16)
        pre = sm + mb_ref[...]
        mi_ref[rows, :] = jnp.where(gate, pltpu.roll(pre, SM_F - SM_I, 1), 0.0)
        mf_ref[rows, :] = jnp.where(gate, _log_sigmoid(pre), 0.0)
    cv_ref[...] = prev


def _inproj(x2d, l, g, w_cat, cos_t, sin_t, wa2p, ba, cw, mb, *, S, tm=TOKEN_TILE, C=MIX_CHUNK):
    n, d = x2d.shape
    widths = [(QK, F32), (QK, F32), (QK, F32), (VW, BF16), (VW, BF16),
              (QK, BF16), (QK, BF16), (QK, BF16), (VW, BF16), (VW, BF16),
              (MLW, BF16), (MLW, BF16), (MLW, BF16), (MLW, BF16), (SMALL_W, F32), (SMALL_W, F32)]
    row = lambda w: pl.BlockSpec((tm, w), lambda i: (i, 0))
    return pl.pallas_call(
        functools.partial(_inproj_kernel, tiles_per_seq=S // tm, C=C),
        out_shape=tuple(jax.ShapeDtypeStruct((n, w), dt) for w, dt in widths),
        grid=(n // tm,),
        in_specs=[row(d), _layer(l, (1, d), 1), _layer(l, (d, W_CAT), 1), row(LANES), row(LANES),
                  _layer(l, (SMALL_W, QK), 1), _layer(l, (1, QK), 1), _layer(l, (ML_CONV, 2 * MLW), 1),
                  _layer(l, (1, SMALL_W), 1)],
        out_specs=tuple(row(w) for w, _ in widths),
        scratch_shapes=[pltpu.VMEM((8, 2 * MLW), F32)],
        compiler_params=_params(("arbitrary",)),
        name="inproj",
    )(x2d, g, w_cat, cos_t, sin_t, wa2p, ba, cw, mb)


def _build_w_cat(w_in):
    offs = np.concatenate([[0], np.cumsum(IN_SIZES)])
    w_in = w_in.astype(BF16)
    col = lambda i: w_in[:, :, offs[i]:offs[i + 1]]
    (gq, gk, gv, ga, gr, rq, rk, rv, rg, mq, mk, mv, mo, mi, mf) = [col(i) for i in range(15)]
    small = jnp.concatenate([ga, mi, mf], axis=-1)
    small = jnp.pad(small, ((0, 0), (0, 0), (0, SMALL_W - small.shape[-1])))
    pieces = [gq[..., :QK0], gk[..., :QK0], gq[..., QK0:], gk[..., QK0:], rq[..., QK0:], rk[..., QK0:],
              rq[..., :QK0], rk[..., :QK0], gv, gr, rv, rg, mq, mk, mv, mo, small]
    w_cat = jnp.concatenate(pieces, axis=-1).astype(BF16)
    assert w_cat.shape[-1] == W_CAT
    return w_cat


def _pair_sum(x, lo):
    s_lo = jnp.sum(jnp.where(lo, x, 0.0), axis=-1, keepdims=True)
    s_hi = jnp.sum(jnp.where(lo, 0.0, x), axis=-1, keepdims=True)
    return jnp.where(lo, s_lo, s_hi)


def _pair_dot(s_a, s_b, v_a, v_b):
    return _dot(jnp.concatenate([s_a, s_b], axis=1), jnp.concatenate([v_a, v_b], axis=0))


def _qk_tiles(q_bf, k_bf, head_w):
    width = q_bf.shape[-1]
    out = []
    for h in range(width // head_w):
        t0 = (h * head_w) // LANES * LANES
        t1 = min(t0 + LANES, width)
        m = _lane_mask(t1 - t0, h * head_w - t0, (h + 1) * head_w - t0)
        out.append((q_bf[:, t0:t1] * m, k_bf[:, t0:t1]))
    return out


def _gla_kernel(q_ref, k_ref, la_ref, v_ref, r_ref, gn_ref, o_ref, st_ref, oacc_ref, *, T, C):
    @pl.when(pl.program_id(1) == 0)
    def _():
        st_ref[...] = jnp.zeros_like(st_ref)

    n_chunks = T // C
    causal = _iota((C, C), 0) >= _iota((C, C), 1)
    tri = causal.astype(BF16)
    bd = (_iota((VW, QK), 0) // GLA_DV) == (_iota((VW, QK), 1) // GLA_DK)
    lo_bf = _lane_mask(LANES, 0, GLA_DV)
    hi_bf = _lane_mask(LANES, GLA_DV, LANES)
    lo = _iota((1, LANES), 1) < GLA_DV
    scale = GLA_DK ** -0.5

    bl_min = jnp.sum(la_ref[0:C, :], axis=0, keepdims=True)
    for c in range(1, n_chunks):
        bl_min = jnp.minimum(bl_min, jnp.sum(la_ref[c * C:(c + 1) * C, :], axis=0, keepdims=True))
    safe = jnp.min(bl_min) > -GLA_SAFE_LOG

    def finish(o, rws, cols):
        ms = _pair_sum(o * o, lo) * (1.0 / GLA_DV)
        return (o * lax.rsqrt(ms + EPS) * gn_ref[:, cols] * r_ref[rws, cols].astype(F32)).astype(BF16)

    @pl.when(safe)
    def _():
        bcs = [_cumsum_rows(tri, la_ref[c * C:(c + 1) * C, :]) for c in range(n_chunks)]
        def front(c):
            rows = slice(c * C, (c + 1) * C)
            bc = bcs[c]
            bl = bc[C - 1:C, :]
            k = k_ref[rows, :]
            qt = (q_ref[rows, :] * (scale * jnp.exp(bc))).astype(BF16)
            kt = (k * jnp.exp(-bc)).astype(BF16)
            kh = (k * jnp.exp(bl - bc)).astype(BF16)
            raw = [_dot_nt(qm, kk) for qm, kk in _qk_tiles(qt, kt, GLA_DK)]
            return rows, bl, qt, raw, _dot_tn(v_ref[rows, :], kh)

        st = st_ref[...]
        nxt = front(0)
        for c in range(n_chunks):
            rows, bl, qt, raw, upd = nxt
            if c + 1 < n_chunks:
                nxt = front(c + 1)
            sc = [s.astype(BF16) * tri for s in raw]
            inter = _dot_nt(qt, st.astype(BF16))
            for p in range(GLA_HEADS // 2):
                cols = slice(LANES * p, LANES * (p + 1))
                vp = v_ref[rows, cols]
                o = _pair_dot(sc[2 * p], sc[2 * p + 1], vp * lo_bf, vp * hi_bf) + inter[:, cols]
                o_ref[rows, cols] = finish(o, rows, cols)
            st = st * jnp.exp(bl) + jnp.where(bd, upd, 0.0)
        st_ref[...] = st

    @pl.when(jnp.logical_not(safe))
    def _():
        vb = v_ref[...]
        row_id = _iota((T, 2 * LANES), 0)

        def token(t, carry):
            a_t = jnp.exp(la_ref[pl.ds(t, 1), :])
            k_t = k_ref[pl.ds(t, 1), :]
            q_t = q_ref[pl.ds(t, 1), :] * scale
            v_col = _dot_tn(vb, (row_id == t).astype(BF16))[:, :QK]
            st = st_ref[...] * a_t + jnp.where(bd, v_col * k_t, 0.0)
            st_ref[...] = st
            q8 = jnp.broadcast_to(q_t, (8, QK)).astype(BF16)
            oacc_ref[pl.ds(t, 1), :] = _dot_nt(q8, st.astype(BF16))[0:1, :]
            return carry

        lax.fori_loop(0, T, token, 0)
        for p in range(GLA_HEADS // 2):
            cols = slice(LANES * p, LANES * (p + 1))
            o_ref[:, cols] = finish(oacc_ref[:, cols], slice(0, T), cols)


def _mix_specs(S, T):
    nt = S // T
    row = lambda w: pl.BlockSpec((T, w), lambda b, i: (b * nt + i, 0))
    full = lambda a, c: pl.BlockSpec((a, c), lambda b, i: (0, 0))
    return nt, row, full


def _gla(q, k, la, v, r, l, gn, *, B, S, T=MIX_TILE_LONG, C=MIX_CHUNK):
    nt, row, full = _mix_specs(S, T)
    return pl.pallas_call(
        functools.partial(_gla_kernel, T=T, C=C),
        out_shape=jax.ShapeDtypeStruct((B * S, VW), BF16),
        grid=(B, nt),
        in_specs=[row(QK), row(QK), row(QK), row(VW), row(VW), _layer(l, (1, VW), 2)],
        out_specs=row(VW),
        scratch_shapes=[pltpu.VMEM((VW, QK), F32), pltpu.VMEM((T, VW), F32)],
        compiler_params=_params(("arbitrary", "arbitrary")),
        name="gla",
    )(q, k, la, v, r, gn)


def _ret_kernel(q_ref, ks_ref, kd_ref, v_ref, rg_ref, gn_ref, o_ref, rt_ref, dec_ref, *, T, C):
    @pl.when(pl.program_id(1) == 0)
    def _():
        rt_ref[...] = jnp.zeros_like(rt_ref)
        rel = (_iota((C, C), 0) - _iota((C, C), 1)).astype(F32)
        for h in range(RET_HEADS):
            log_g = float(np.log(1.0 - 2.0 ** (-5.0 - h)))
            dec_ref[h] = jnp.where(rel >= 0.0, jnp.exp(log_g * jnp.maximum(rel, 0.0)), 0.0)

    lo_bf = _lane_mask(LANES, 0, RET_DV)
    hi_bf = _lane_mask(LANES, RET_DV, LANES)
    lo = _iota((1, LANES), 1) < RET_DV
    idx = _iota((C, 1), 0).astype(F32)
    q_dec = jnp.exp(_ret_log_gamma(VW, RET_DV) * (idx + 1.0))
    chunk_dec = jnp.exp(_ret_log_gamma(QK, RET_DK) * float(C))
    bd = (_iota((VW, QK), 0) // RET_DV) == (_iota((VW, QK), 1) // RET_DK)

    def front(c):
        rows = slice(c * C, (c + 1) * C)
        q = q_ref[rows, :]
        raw = [_dot_nt(qm, kk) for qm, kk in _qk_tiles(q, ks_ref[rows, :], RET_DK)]
        return rows, q, raw, _dot_tn(v_ref[rows, :], kd_ref[rows, :])

    rt = rt_ref[...]
    n_chunks = T // C
    nxt = front(0)
    for c in range(n_chunks):
        rows, q, raw, upd = nxt
        if c + 1 < n_chunks:
            nxt = front(c + 1)
        sc = [(s * dec_ref[h]).astype(BF16) for h, s in enumerate(raw)]
        inter = _dot_nt(q, rt.astype(BF16)) * q_dec
        for p in range(RET_HEADS // 2):
            cols = slice(LANES * p, LANES * (p + 1))
            vp = v_ref[rows, cols]
            o = _pair_dot(sc[2 * p], sc[2 * p + 1], vp * lo_bf, vp * hi_bf) + inter[:, cols]
            mu = _pair_sum(o, lo) * (1.0 / RET_DV)
            xc = o - mu
            var = _pair_sum(xc * xc, lo) * (1.0 / RET_DV)
            y = xc * lax.rsqrt(var + EPS) * gn_ref[:, cols] * rg_ref[rows, cols].astype(F32)
            o_ref[rows, cols] = y.astype(BF16)
        rt = rt * chunk_dec + jnp.where(bd, upd, 0.0)
    rt_ref[...] = rt


def _ret(q, ks, kd, v, rg, l, gn, *, B, S, T=MIX_TILE, C=MIX_CHUNK):
    nt, row, full = _mix_specs(S, T)
    return pl.pallas_call(
        functools.partial(_ret_kernel, T=T, C=C),
        out_shape=jax.ShapeDtypeStruct((B * S, VW), BF16),
        grid=(B, nt),
        in_specs=[row(QK), row(QK), row(QK), row(VW), row(VW), _layer(l, (1, VW), 2)],
        out_specs=row(VW),
        scratch_shapes=[pltpu.VMEM((VW, QK), F32), pltpu.VMEM((RET_HEADS, C, C), F32)],
        compiler_params=_params(("arbitrary", "arbitrary")),
        name="retention",
    )(q, ks, kd, v, rg, gn)


def _mlstm_kernel(q_ref, k_ref, v_ref, og_ref, mi_ref, mf_ref, o_ref, ct_ref, n_ref, m_ref, *, T, C):
    @pl.when(pl.program_id(1) == 0)
    def _():
        ct_ref[...] = jnp.zeros_like(ct_ref)
        n_ref[...] = jnp.zeros_like(n_ref)
        m_ref[...] = jnp.zeros_like(m_ref)

    causal = _iota((C, C), 0) >= _iota((C, C), 1)
    tri = causal.astype(BF16)
    bd = (_iota((MLW, MLW), 0) // ML_DV) == (_iota((MLW, MLW), 1) // ML_DK)
    bd_bf = bd.astype(BF16)
    lo_bf = _lane_mask(LANES, 0, ML_DV)
    hi_bf = _lane_mask(LANES, ML_DV, LANES)
    lo = _iota((1, LANES), 1) < ML_DV
    widen = (_iota((LANES, MLW), 0) == (_iota((LANES, MLW), 1) // ML_DV + SM_F)).astype(BF16)
    ones_blk = jnp.ones((C, LANES), BF16)
    row_id = _iota((C, LANES), 0)

    n_chunks = T // C
    rows = [slice(c * C, (c + 1) * C) for c in range(n_chunks)]
    i_pre = [mi_ref[r, :] for r in rows]
    f_cum = [_cumsum_rows(tri, mf_ref[r, :]) for r in rows]
    qs = [q_ref[r, :] for r in rows]
    ks = [k_ref[r, :] for r in rows]
    raw = [[_dot_nt(qm, kk) for qm, kk in _qk_tiles(q, k, ML_DK)] for q, k in zip(qs, ks)]

    g_t, g_max, b_last, log_w, w_max = [], [], [], [], []
    for c in range(n_chunks):
        g = i_pre[c] - f_cum[c]
        g_t.append(g.T)
        run = g
        sft = 1
        while sft < C:
            run = jnp.maximum(run, jnp.where(row_id >= sft, pltpu.roll(run, sft, 0), -jnp.inf))
            sft *= 2
        g_max.append(run)
        b_last.append(f_cum[c][C - 1:C, :])
        log_w.append(b_last[c] - f_cum[c] + i_pre[c])
        w_max.append(jnp.max(log_w[c], axis=0, keepdims=True))
    ms = [m_ref[...]]
    for c in range(n_chunks):
        ms.append(jnp.maximum(b_last[c] + ms[c], w_max[c]))

    mm, wide, gf = [], [], []
    for c in range(n_chunks):
        mm.append(jnp.maximum(g_max[c], ms[c]))
        slab = jnp.concatenate([jnp.exp(ms[c] - mm[c]), jnp.exp(-(f_cum[c] + mm[c])), jnp.exp(log_w[c] - ms[c + 1])],
                               axis=0)
        wide.append(_dot(slab.astype(BF16), widen))
        g_hi, g_lo = _split2(jnp.broadcast_to(jnp.exp(b_last[c] + ms[c] - ms[c + 1]), (8, LANES)))
        gf.append((_dot(g_hi, widen) + _dot(g_lo, widen))[0:1])
    w_inter = [w[0:C] for w in wide]
    e_negm = [w[C:2 * C] for w in wide]
    wk = [w[2 * C:3 * C] * k.astype(F32) for w, k in zip(wide, ks)]
    upd = [_dot_tn(v_ref[r, :], x.astype(BF16)) for r, x in zip(rows, wk)]

    cts, ns = [ct_ref[...]], [n_ref[...]]
    for c in range(n_chunks):
        cts.append(cts[c] * gf[c] + jnp.where(bd, upd[c], 0.0))
        ns.append(ns[c] * gf[c] + jnp.sum(wk[c], axis=0, keepdims=True))
    inter = [_dot_nt(qs[c], jnp.concatenate(
        [cts[c].astype(BF16), jnp.broadcast_to(ns[c].astype(BF16), (MLW, MLW)) * bd_bf], axis=0))
        for c in range(n_chunks)]

    res = []
    for c in range(n_chunks):
        per_head = []
        for h in range(ML_HEADS):
            lane = SM_F + h
            dm = jnp.where(causal, jnp.exp(g_t[c][lane:lane + 1, :] - mm[c][:, lane:lane + 1]), 0.0)
            s = (raw[c][h] * dm).astype(BF16)
            vp = v_ref[rows[c], LANES * (h // 2):LANES * (h // 2 + 1)]
            v_ext = jnp.concatenate([vp * (lo_bf if h % 2 == 0 else hi_bf), ones_blk], axis=1)
            per_head.append(_dot(s, v_ext))
        res.append(per_head)
    for c in range(n_chunks):
        for p in range(ML_HEADS // 2):
            cols = slice(LANES * p, LANES * (p + 1))
            ra, rb = res[c][2 * p], res[c][2 * p + 1]
            num = ra[:, :LANES] + rb[:, :LANES] + w_inter[c][:, cols] * inter[c][:, cols]
            den = (jnp.where(lo, ra[:, LANES:], rb[:, LANES:])
                   + w_inter[c][:, cols] * inter[c][:, MLW + LANES * p:MLW + LANES * (p + 1)])
            h_t = num / jnp.maximum(jnp.abs(den), e_negm[c][:, cols])
            o_ref[rows[c], cols] = (og_ref[rows[c], cols].astype(F32) * h_t).astype(BF16)
    ct_ref[...] = cts[n_chunks]
    n_ref[...] = ns[n_chunks]
    m_ref[...] = ms[n_chunks]


def _mlstm(q, k, v, og, mi, mf, *, B, S, T=MIX_TILE_LONG, C=MIX_CHUNK):
    nt, row, full = _mix_specs(S, T)
    return pl.pallas_call(
        functools.partial(_mlstm_kernel, T=T, C=C),
        out_shape=jax.ShapeDtypeStruct((B * S, MLW), BF16),
        grid=(B, nt),
        in_specs=[row(MLW), row(MLW), row(MLW), row(MLW), row(SMALL_W), row(SMALL_W)],
        out_specs=row(MLW),
        scratch_shapes=[pltpu.VMEM((MLW, MLW), F32), pltpu.VMEM((1, MLW), F32), pltpu.VMEM((1, LANES), F32)],
        compiler_params=_params(("arbitrary", "arbitrary")),
        name="mlstm",
    )(q, k, v, og, mi, mf)


def _memkv_kernel(mem_ref, g_ref, w_ref, k_ref, v_ref):
    mn = _rms(mem_ref[0], g_ref[...]).astype(BF16)
    d = mem_ref.shape[-1]
    k_ref[0] = _dot(mn, w_ref[:, :d]).astype(BF16)
    v_ref[0] = _dot(mn, w_ref[:, d:]).astype(BF16)


def _memkv(mem, l, g, w_kv):
    b, m, d = mem.shape
    blk = pl.BlockSpec((1, m, d), lambda i: (i, 0, 0))
    return pl.pallas_call(
        _memkv_kernel,
        out_shape=(jax.ShapeDtypeStruct((b, m, d), BF16), jax.ShapeDtypeStruct((b, m, d), BF16)),
        grid=(b,),
        in_specs=[blk, _layer(l, (1, d), 1), _layer(l, (d, 2 * d), 1)],
        out_specs=(blk, blk),
        compiler_params=_params(("parallel",)),
        name="mem_kv",
    )(mem, g, w_kv)


OUTXA_SUB = 256


def _outxa_kernel(x_ref, og_ref, or_ref, om_ref, wo_ref, g_ref, wq_ref, k_ref, v_ref, wxo_ref, o_ref):
    subs = [slice(r * OUTXA_SUB, (r + 1) * OUTXA_SUB) for r in range(x_ref.shape[0] // OUTXA_SUB)]
    x1 = [x_ref[r, :] + _dot(jnp.concatenate([og_ref[r, :], or_ref[r, :], om_ref[r, :]], axis=-1), wo_ref[...])
          for r in subs]
    hn = [_rms(t, g_ref[...]).astype(BF16) for t in x1]
    q = [(_dot(t, wq_ref[...]) * (XA_DH ** -0.5)).astype(BF16) for t in hn]
    heads = [slice(XA_DH * h, XA_DH * (h + 1)) for h in range(XA_HEADS)]
    s = [[_dot_nt(t[:, c], k_ref[0, :, c]) for c in heads] for t in q]
    p = []
    for per_head in s:
        e = [jnp.exp(t - jnp.max(t, axis=-1, keepdims=True)) for t in per_head]
        p.append([(t / jnp.sum(t, axis=-1, keepdims=True)).astype(BF16) for t in e])
    o = [jnp.concatenate([_dot(t, v_ref[0, :, c]).astype(BF16) for t, c in zip(per_head, heads)], axis=-1)
         for per_head in p]
    for r, t, u in zip(subs, x1, o):
        o_ref[r, :] = t + _dot(u, wxo_ref[...])


def _outxa(x2d, o_gla, o_ret, o_ml, l, w_out, g_xa, w_q, mem_k, mem_v, w_o, *, B, S, tm=TOKEN_TILE):
    n, d = x2d.shape
    nt = S // tm
    m = mem_k.shape[1]
    row = lambda w: pl.BlockSpec((tm, w), lambda b, i: (b * nt + i, 0))
    kv = pl.BlockSpec((1, m, d), lambda b, i: (b, 0, 0))
    return pl.pallas_call(
        _outxa_kernel,
        out_shape=jax.ShapeDtypeStruct((n, d), F32),
        grid=(B, nt),
        in_specs=[row(d), row(VW), row(VW), row(MLW), _layer(l, (d, d), 2), _layer(l, (1, d), 2),
                  _layer(l, (d, d), 2), kv, kv, _layer(l, (d, d), 2)],
        out_specs=row(d),
        compiler_params=_params(("parallel", "parallel")),
        name="outproj_xattn",
    )(x2d, o_gla, o_ret, o_ml, w_out, g_xa, w_q, mem_k, mem_v, w_o)


def kernel(x, mem, positions, g_ffa, w_ffa_gu, w_ffa_down, g_mix, w_in, gla_w_a2, gla_b_a, gla_g_norm,
           ret_g_norm, ml_conv, ml_b_i, ml_b_f, w_out, g_xa, g_mem, w_xa_q, w_xa_kv, w_xa_o, g_ffb,
           w_ffb_gu, w_ffb_down, g_final):
    B, S, D = x.shape
    L = w_in.shape[0]
    N = B * S
    bf = lambda t: t.astype(BF16)
    vec = lambda t: t[:, None, :]
    w_ffa_gu, w_ffa_down, w_ffb_gu, w_ffb_down = bf(w_ffa_gu), bf(w_ffa_down), bf(w_ffb_gu), bf(w_ffb_down)
    w_out_b, w_q_b, w_kv_b, w_o_b = bf(w_out), bf(w_xa_q), bf(w_xa_kv), bf(w_xa_o)
    w_cat = _build_w_cat(w_in)
    wa2p = bf(jnp.pad(gla_w_a2, ((0, 0), (0, SMALL_W - GLA_RANK), (0, 0))))
    ml_bias = jnp.pad(jnp.concatenate([ml_b_i, ml_b_f], axis=-1),
                      ((0, 0), (SM_I, SMALL_W - SM_I - 2 * ML_HEADS)))
    g_ffa, g_mix, g_xa, g_mem, g_ffb = vec(g_ffa), vec(g_mix), vec(g_xa), vec(g_mem), vec(g_ffb)
    gla_b_a, gla_g_norm, ret_g_norm, ml_bias = vec(gla_b_a), vec(gla_g_norm), vec(ret_g_norm), vec(ml_bias)
    g_final = g_final[None, :]

    cos_t, sin_t = _rope_tables(positions.reshape(1, N))
    h = x.reshape(N, D)
    for l in range(L):
        h = _ffn(h, l, g_ffa, w_ffa_gu, w_ffa_down, g_final, final=False)
        (gq, gk, gla, gv, gr, rq, rks, rkd, rv, rg, mq, mk, mv, mo, mi, mf) = _inproj(
            h, l, g_mix, w_cat, cos_t, sin_t, wa2p, gla_b_a, ml_conv, ml_bias, S=S)
        o_gla = _gla(gq, gk, gla, gv, gr, l, gla_g_norm, B=B, S=S)
        o_ret = _ret(rq, rks, rkd, rv, rg, l, ret_g_norm, B=B, S=S)
        o_ml = _mlstm(mq, mk, mv, mo, mi, mf, B=B, S=S)
        mem_k, mem_v = _memkv(mem, l, g_mem, w_kv_b)
        h = _outxa(h, o_gla, o_ret, o_ml, l, w_out_b, g_xa, w_q_b, mem_k, mem_v, w_o_b, B=B, S=S)
        h = _ffn(h, l, g_ffb, w_ffb_gu, w_ffb_down, g_final, final=(l == L - 1))
    return h.reshape(B, S, D)
```
